```python
import math
import jax, jax.numpy as jnp
from jax import lax
import numpy as np

D_MODEL = 2048
BATCH = 1
SEQ = 8192
DEPTH = 2
DEC_BATCH = 128
DEC_SEQ = 8
PAST_LEN = 8192
PAGE_SIZE = 128

HEAD_DIM = 64
BLOCK = 128
EPS = 1e-6
A_PATTERN = ((128, 1), (512, 4), (2048, 16))
A_GROUPS = 3
A_HEADS = D_MODEL // 512
A_WIDTH = A_GROUPS * A_HEADS * HEAD_DIM
A_OUT = A_HEADS * HEAD_DIM
SSM_INNER = D_MODEL // 2
SSM_HEADS = SSM_INNER // HEAD_DIM
SSM_GROUPS = 2
SSM_STATE = 128
CONV_WIDTH = 4
CONV_DIM = SSM_INNER + 2 * SSM_GROUPS * SSM_STATE
SSD_CHUNK = 128
C_HEADS = 3 * D_MODEL // 512
C_KV_HEADS = C_HEADS // 4
C_WINDOW = 128
N_BRANCH = 3
IN_WIDTHS = (A_WIDTH, A_WIDTH, A_WIDTH,
             C_HEADS * HEAD_DIM, C_KV_HEADS * HEAD_DIM, C_KV_HEADS * HEAD_DIM,
             SSM_INNER, CONV_DIM, SSM_HEADS, N_BRANCH * D_MODEL)
IN_WIDTH = sum(IN_WIDTHS)
N_EXPERTS = 64
N_EXPERT_GROUPS = 8
TOPK_GROUPS = 4
TOP_K = 8
MOE_HIDDEN = D_MODEL // 4
ROUTED_SCALE = 2.5
EXPERT_BLOCK = 64

kernel_name = 'hybrid_gated_dilated_ssd_swa_moe_step'


def rms_norm(x, g):
    xf = x.astype(jnp.float32)
    y = xf * lax.rsqrt(jnp.mean(xf * xf, axis=-1, keepdims=True) + EPS)
    return y.astype(x.dtype) * g


def softmax_lse(s, sink=None):
    m = jnp.max(s, axis=-1, keepdims=True)
    if sink is not None:
        m = jnp.maximum(m, sink)
    e = jnp.exp(s - m)
    den = jnp.sum(e, axis=-1, keepdims=True)
    if sink is not None:
        den = den + jnp.exp(sink - m)
    return e / den, (m + jnp.log(den))[..., 0]


def banded_attn(q, k, v, window, sink=None):
    b, n, hq, hd = q.shape
    hk = k.shape[2]
    g = hq // hk
    nb = -(-n // BLOCK)
    padw = ((0, 0), (0, nb * BLOCK - n), (0, 0), (0, 0))
    qb = jnp.pad(q, padw).reshape(b, nb, BLOCK, hk, g, hd)

    def with_prev(t):
        t = jnp.pad(t, padw).reshape(b, nb, BLOCK, hk, hd)
        prev = jnp.pad(t, ((0, 0), (1, 0), (0, 0), (0, 0), (0, 0)))[:, :-1]
        return jnp.concatenate([prev, t], axis=2)

    kk, vv = with_prev(k), with_prev(v)
    s = jnp.einsum('bnqkgd,bnskd->bnkgqs', qb, kk).astype(jnp.float32) / math.sqrt(hd)
    dist = jnp.arange(BLOCK)[:, None] + BLOCK - jnp.arange(2 * BLOCK)[None, :]
    band = (dist >= 0) & (dist <= window)
    has_prev = (jnp.arange(nb) > 0)[:, None, None] | (jnp.arange(2 * BLOCK) >= BLOCK)[None, None, :]
    valid = band[None] & has_prev
    s = jnp.where(valid[None, :, None, None], s, -jnp.inf)
    sk = None if sink is None else sink.astype(jnp.float32).reshape(1, 1, hk, g, 1, 1)
    pr, lse = softmax_lse(s, sk)
    o = jnp.einsum('bnkgqs,bnskd->bnqkgd', pr.astype(v.dtype), vv).reshape(b, nb * BLOCK, hq, hd)
    lse = lse.transpose(0, 1, 4, 2, 3).reshape(b, nb * BLOCK, hq)
    return o[:, :n], lse[:, :n]


def combine_groups(outs, lses):
    o = jnp.stack(outs, axis=0)
    wts = jax.nn.softmax(jnp.stack(lses, axis=0), axis=0)
    return jnp.einsum('gblh,gblhd->blhd', wts.astype(o.dtype), o)


def dilated_prompt(q, k, v):
    b, L = q.shape[:2]
    outs, lses, states = [], [], []
    for gi, (w, d) in enumerate(A_PATTERN):
        def to_res(t):
            return (t[:, :, gi].reshape(b, L // d, d, A_HEADS, HEAD_DIM)
                    .transpose(0, 2, 1, 3, 4).reshape(b * d, L // d, A_HEADS, HEAD_DIM))
        o, lse = banded_attn(to_res(q), to_res(k), to_res(v), w // d)
        outs.append(o.reshape(b, d, L // d, A_HEADS, HEAD_DIM).transpose(0, 2, 1, 3, 4)
                    .reshape(b, L, A_HEADS, HEAD_DIM))
        lses.append(lse.reshape(b, d, L // d, A_HEADS).transpose(0, 2, 1, 3).reshape(b, L, A_HEADS))
        keep = min(w, L)
        states.append(jnp.stack([k[:, L - keep:, gi], v[:, L - keep:, gi]], axis=2))
    return combine_groups(outs, lses), states


def dilated_step(q, k, v, bufs):
    b, S = q.shape[:2]
    outs, lses, states = [], [], []
    for gi, ((w, d), buf) in enumerate(zip(A_PATTERN, bufs)):
        lb = buf.shape[1]
        kv = jnp.concatenate([buf.astype(k.dtype), jnp.stack([k[:, :, gi], v[:, :, gi]], axis=2)], axis=1)
        idx = lb + jnp.arange(S)[:, None] - d * jnp.arange(w // d + 1)[None, :]
        valid = idx >= 0
        kvg = kv[:, jnp.maximum(idx, 0)]
        s = jnp.einsum('bshd,bsjhd->bhsj', q[:, :, gi], kvg[:, :, :, 0]).astype(jnp.float32) / math.sqrt(HEAD_DIM)
        s = jnp.where(valid, s, -jnp.inf)
        pr, lse = softmax_lse(s)
        outs.append(jnp.einsum('bhsj,bsjhd->bshd', pr.astype(v.dtype), kvg[:, :, :, 1]))
        lses.append(lse.transpose(0, 2, 1))
        states.append(kv[:, S:])
    return combine_groups(outs, lses), states


def swa_step(q, k, v, buf, sink):
    b, S = q.shape[:2]
    lb = buf.shape[1]
    g = C_HEADS // C_KV_HEADS
    kv = jnp.concatenate([buf.astype(k.dtype), jnp.stack([k, v], axis=2)], axis=1)
    qg = q.reshape(b, S, C_KV_HEADS, g, HEAD_DIM)
    s = jnp.einsum('bskgd,btkd->bkgst', qg, kv[:, :, 0]).astype(jnp.float32) / math.sqrt(HEAD_DIM)
    dist = (lb + jnp.arange(S))[:, None] - jnp.arange(lb + S)[None, :]
    s = jnp.where((dist >= 0) & (dist <= C_WINDOW), s, -jnp.inf)
    pr, _ = softmax_lse(s, sink.astype(jnp.float32).reshape(1, C_KV_HEADS, g, 1, 1))
    o = jnp.einsum('bkgst,btkd->bskgd', pr.astype(v.dtype), kv[:, :, 1]).reshape(b, S, C_HEADS, HEAD_DIM)
    return o, kv[:, S:]


def ssd_scan(x, dt, a, bm, cm, h0, chunk):
    b, L = x.shape[:2]
    nc = L // chunk
    tri = jnp.tril(jnp.ones((chunk, chunk), bool))

    def split(t):
        return jnp.moveaxis(t.reshape(b, nc, chunk, *t.shape[2:]), 1, 0)

    def step(h, inp):
        xc, dtc, bc, cc = inp
        xf, bf, cf = xc.astype(jnp.float32), bc.astype(jnp.float32), cc.astype(jnp.float32)
        acs = jnp.cumsum(dtc * a, axis=1)
        seg = acs[:, :, None] - acs[:, None, :]
        decay = jnp.exp(jnp.where(tri[None, :, :, None, None], seg, -jnp.inf))
        cb = jnp.einsum('bign,bjgn->bijg', cf, bf)
        y = jnp.einsum('bijg,bijgh,bjgh,bjghp->bighp', cb, decay, dtc, xf)
        y = y + jnp.einsum('bign,bghpn->bighp', cf, h) * jnp.exp(acs)[..., None]
        tail = jnp.exp(acs[:, -1:] - acs) * dtc
        h = h * jnp.exp(acs[:, -1])[..., None, None] + jnp.einsum('bjgh,bjghp,bjgn->bghpn', tail, xf, bf)
        return h, y

    h_last, ys = lax.scan(step, h0, (split(x), split(dt), split(bm), split(cm)))
    y = jnp.moveaxis(ys, 0, 1).reshape(b, L, *x.shape[2:])
    return y.astype(x.dtype), h_last


def ssm_branch(z, xbc, dt_raw, conv_buf, h0, p, chunk):
    b, L, _ = xbc.shape
    hpg = SSM_HEADS // SSM_GROUPS
    xpad = jnp.concatenate([conv_buf.astype(xbc.dtype), xbc], axis=1)
    conv = lax.conv_general_dilated(xpad, p['conv_w'][:, None, :], window_strides=(1,), padding='VALID',
                                    dimension_numbers=('NWC', 'WIO', 'NWC'), feature_group_count=CONV_DIM)
    u = jax.nn.silu(conv + p['conv_b'])
    xs = u[..., :SSM_INNER].reshape(b, L, SSM_GROUPS, hpg, HEAD_DIM)
    bm = u[..., SSM_INNER:SSM_INNER + SSM_GROUPS * SSM_STATE].reshape(b, L, SSM_GROUPS, SSM_STATE)
    cm = u[..., SSM_INNER + SSM_GROUPS * SSM_STATE:].reshape(b, L, SSM_GROUPS, SSM_STATE)
    dt = jax.nn.softplus(dt_raw.astype(jnp.float32) + p['dt_bias'].astype(jnp.float32)).reshape(b, L, SSM_GROUPS, hpg)
    a = -jnp.exp(p['a_log'].astype(jnp.float32)).reshape(SSM_GROUPS, hpg)
    y, h_last = ssd_scan(xs, dt, a, bm, cm, h0.astype(jnp.float32), chunk)
    y = y + p['d_skip'].reshape(SSM_GROUPS, hpg, 1) * xs
    yg = (y.reshape(b, L, SSM_INNER) * jax.nn.silu(z)).reshape(b, L, SSM_GROUPS, SSM_INNER // SSM_GROUPS)
    y = rms_norm(yg, p['ssm_norm'].reshape(SSM_GROUPS, -1)).reshape(b, L, SSM_INNER)
    return y, h_last, xpad[:, L:]


def routed_experts(h, topi, topw, w_gate, w_up, w_down):
    T, D = h.shape
    E = w_gate.shape[0]
    K = topi.shape[1]
    A = T * K
    flat_e = topi.reshape(A)
    order = jnp.argsort(flat_e)
    sorted_e = flat_e[order]
    counts = jnp.zeros(E, jnp.int32).at[flat_e].add(1)
    padded = (counts + EXPERT_BLOCK - 1) // EXPERT_BLOCK * EXPERT_BLOCK
    pad_end = jnp.cumsum(padded)
    pad_start = pad_end - padded
    start = jnp.cumsum(counts) - counts
    dest = pad_start[sorted_e] + jnp.arange(A, dtype=jnp.int32) - start[sorted_e]
    n_blocks = (A + E * (EXPERT_BLOCK - 1)) // EXPERT_BLOCK + 1
    P = n_blocks * EXPERT_BLOCK
    row_tok = jnp.zeros(P, jnp.int32).at[dest].set((order // K).astype(jnp.int32))
    row_w = jnp.zeros(P, h.dtype).at[dest].set(topw.reshape(A)[order])
    block_e = jnp.minimum(jnp.searchsorted(pad_end, jnp.arange(n_blocks, dtype=jnp.int32) * EXPERT_BLOCK, side='right'), E - 1)
    xb = h[row_tok].reshape(n_blocks, EXPERT_BLOCK, D)

    def expert_block(args):
        xblk, e = args
        return (jax.nn.silu(xblk @ w_gate[e]) * (xblk @ w_up[e])) @ w_down[e]

    yb = lax.map(expert_block, (xb, block_e)).reshape(P, D)
    return jnp.zeros((T, D), h.dtype).at[row_tok].add(yb * row_w[:, None])


def moe_ffn(h, p):
    T = h.shape[0]
    scores = jax.nn.sigmoid((h @ p['router_w']).astype(jnp.float32))
    sel = scores + p['router_bias'].astype(jnp.float32)
    grp_score = lax.top_k(sel.reshape(T, N_EXPERT_GROUPS, N_EXPERTS // N_EXPERT_GROUPS), 2)[0].sum(-1)
    _, top_g = lax.top_k(grp_score, TOPK_GROUPS)
    gmask = jax.nn.one_hot(top_g, N_EXPERT_GROUPS).sum(axis=1) > 0
    emask = jnp.repeat(gmask, N_EXPERTS // N_EXPERT_GROUPS, axis=1)
    _, topi = lax.top_k(jnp.where(emask, sel, -jnp.inf), TOP_K)
    w = jnp.take_along_axis(scores, topi, axis=-1)
    w = (w / jnp.sum(w, axis=-1, keepdims=True) * ROUTED_SCALE).astype(h.dtype)
    routed = routed_experts(h, topi, w, p['exp_gate'], p['exp_up'], p['exp_down'])
    shared = (jax.nn.silu(h @ p['shared_gate']) * (h @ p['shared_up'])) @ p['shared_down']
    return routed + shared


def token_mixers(h, p, past):
    b, L, _ = h.shape
    offs = np.cumsum(IN_WIDTHS)[:-1].tolist()
    qa, ka, va, qc, kc, vc, z, xbc, dt_raw, gates = jnp.split(h @ p['w_in'], offs, axis=-1)
    qa, ka, va = (t.reshape(b, L, A_GROUPS, A_HEADS, HEAD_DIM) for t in (qa, ka, va))
    qc = qc.reshape(b, L, C_HEADS, HEAD_DIM)
    kc = kc.reshape(b, L, C_KV_HEADS, HEAD_DIM)
    vc = vc.reshape(b, L, C_KV_HEADS, HEAD_DIM)
    if past is None:
        oa, a_state = dilated_prompt(qa, ka, va)
        oc, _ = banded_attn(qc, kc, vc, C_WINDOW, p['sinks'])
        keep = min(C_WINDOW, L)
        c_state = jnp.stack([kc[:, L - keep:], vc[:, L - keep:]], axis=2)
        conv_buf = jnp.zeros((b, CONV_WIDTH - 1, CONV_DIM), xbc.dtype)
        h0 = jnp.zeros((b, SSM_GROUPS, SSM_HEADS // SSM_GROUPS, HEAD_DIM, SSM_STATE), jnp.float32)
        chunk = min(SSD_CHUNK, L)
    else:
        a1, a2, a3, c_buf, h0, conv_buf = past
        oa, a_state = dilated_step(qa, ka, va, (a1, a2, a3))
        oc, c_state = swa_step(qc, kc, vc, c_buf, p['sinks'])
        chunk = L
    ob, h_last, conv_state = ssm_branch(z, xbc, dt_raw, conv_buf, h0, p, chunk)
    g = jax.nn.sigmoid(gates.astype(jnp.float32)).astype(h.dtype).reshape(b, L, N_BRANCH, D_MODEL)
    merged = (g[:, :, 0] * (oa.reshape(b, L, A_OUT) @ p['w_br_a'])
              + g[:, :, 1] * (ob @ p['w_br_b'])
              + g[:, :, 2] * (oc.reshape(b, L, C_HEADS * HEAD_DIM) @ p['w_br_c']))
    return merged @ p['w_out'], (*a_state, c_state, h_last, conv_state)


def layer(x, c, p, past):
    b, L, _ = x.shape
    mod = (jax.nn.silu(c) @ p['ada_w'] + p['ada_b'])[:, None, :]
    sh_m, sc_m, ga_m, sh_f, sc_f, ga_f = jnp.split(mod, 6, axis=-1)
    mix, state = token_mixers(rms_norm(x, p['norm_pre_mix']) * (1 + sc_m) + sh_m, p, past)
    x = x + ga_m * rms_norm(mix, p['norm_post_mix'])
    hf = rms_norm(x, p['norm_pre_ffn']) * (1 + sc_f) + sh_f
    f = moe_ffn(hf.reshape(b * L, D_MODEL), p).reshape(b, L, D_MODEL)
    x = x + ga_f * rms_norm(f, p['norm_post_ffn'])
    return x, state


def setup_inputs(seed: int = 0) -> dict:
    key = jax.random.key(seed)
    ks = iter(jax.random.split(key, 48))

    def nrm(shape, scale=1.0):
        return jax.random.normal(next(ks), shape, jnp.float32) * scale

    def uni(shape, lo, hi):
        return jax.random.uniform(next(ks), shape, jnp.float32, lo, hi)

    D = D_MODEL
    hpg = SSM_HEADS // SSM_GROUPS
    la = [min(w, PAST_LEN) for w, _ in A_PATTERN]
    lc = min(C_WINDOW, PAST_LEN)
    dt0 = jnp.exp(uni((DEPTH, SSM_HEADS), math.log(1e-3), math.log(1e-1)))
    return {
        'x_prompt': nrm((BATCH, SEQ, D)),
        'x_sample': nrm((DEC_BATCH, DEC_SEQ, D)),
        'cache_a1_kv': nrm((DEPTH, DEC_BATCH, la[0], 2, A_HEADS, HEAD_DIM)),
        'cache_a2_kv': nrm((DEPTH, DEC_BATCH, la[1], 2, A_HEADS, HEAD_DIM)),
        'cache_a3_kv': nrm((DEPTH, DEC_BATCH, la[2], 2, A_HEADS, HEAD_DIM)),
        'cache_c_kv': nrm((DEPTH, DEC_BATCH, lc, 2, C_KV_HEADS, HEAD_DIM)),
        'state_ssm': nrm((DEPTH, DEC_BATCH, SSM_GROUPS, hpg, HEAD_DIM, SSM_STATE), 0.3),
        'state_conv': nrm((DEPTH, DEC_BATCH, CONV_WIDTH - 1, CONV_DIM)),
        'c_prompt': nrm((BATCH, D)),
        'c_sample': nrm((DEC_BATCH, D)),
        'ada_w': nrm((DEPTH, D, 6 * D), 0.5 * D ** -0.5),
        'ada_b': nrm((DEPTH, 6 * D), 0.02),
        'norm_pre_mix': 1.0 + nrm((DEPTH, D), 0.1),
        'norm_post_mix': 1.0 + nrm((DEPTH, D), 0.1),
        'norm_pre_ffn': 1.0 + nrm((DEPTH, D), 0.1),
        'norm_post_ffn': 1.0 + nrm((DEPTH, D), 0.1),
        'w_in': nrm((DEPTH, D, IN_WIDTH), D ** -0.5),
        'conv_w': nrm((DEPTH, CONV_WIDTH, CONV_DIM), CONV_WIDTH ** -0.5),
        'conv_b': nrm((DEPTH, CONV_DIM), 0.02),
        'dt_bias': dt0 + jnp.log(-jnp.expm1(-dt0)),
        'a_log': jnp.log(uni((DEPTH, SSM_HEADS), 1.0, 16.0)),
        'd_skip': 1.0 + nrm((DEPTH, SSM_HEADS), 0.1),
        'ssm_norm': 1.0 + nrm((DEPTH, SSM_INNER), 0.1),
        'sinks': nrm((DEPTH, C_HEADS)),
        'w_br_a': nrm((DEPTH, A_OUT, D), A_OUT ** -0.5),
        'w_br_b': nrm((DEPTH, SSM_INNER, D), SSM_INNER ** -0.5),
        'w_br_c': nrm((DEPTH, C_HEADS * HEAD_DIM, D), (C_HEADS * HEAD_DIM) ** -0.5),
        'w_out': nrm((DEPTH, D, D), D ** -0.5),
        'router_w': nrm((DEPTH, D, N_EXPERTS), D ** -0.5),
        'router_bias': nrm((DEPTH, N_EXPERTS), 0.01),
        'exp_gate': nrm((DEPTH, N_EXPERTS, D, MOE_HIDDEN), D ** -0.5),
        'exp_up': nrm((DEPTH, N_EXPERTS, D, MOE_HIDDEN), D ** -0.5),
        'exp_down': nrm((DEPTH, N_EXPERTS, MOE_HIDDEN, D), MOE_HIDDEN ** -0.5),
        'shared_gate': nrm((DEPTH, D, MOE_HIDDEN), D ** -0.5),
        'shared_up': nrm((DEPTH, D, MOE_HIDDEN), D ** -0.5),
        'shared_down': nrm((DEPTH, MOE_HIDDEN, D), MOE_HIDDEN ** -0.5),
    }


def reference(x_prompt, x_sample, cache_a1_kv, cache_a2_kv, cache_a3_kv, cache_c_kv, state_ssm, state_conv,
              c_prompt, c_sample, ada_w, ada_b, norm_pre_mix, norm_post_mix, norm_pre_ffn, norm_post_ffn,
              w_in, conv_w, conv_b, dt_bias, a_log, d_skip, ssm_norm, sinks, w_br_a, w_br_b, w_br_c, w_out,
              router_w, router_bias, exp_gate, exp_up, exp_down, shared_gate, shared_up, shared_down):
    y_p, y_s = x_prompt, x_sample
    new_p, new_s = [], []
    for l in range(DEPTH):
        p = {'ada_w': ada_w[l], 'ada_b': ada_b[l],
             'norm_pre_mix': norm_pre_mix[l], 'norm_post_mix': norm_post_mix[l],
             'norm_pre_ffn': norm_pre_ffn[l], 'norm_post_ffn': norm_post_ffn[l],
             'w_in': w_in[l], 'conv_w': conv_w[l], 'conv_b': conv_b[l], 'dt_bias': dt_bias[l],
             'a_log': a_log[l], 'd_skip': d_skip[l], 'ssm_norm': ssm_norm[l], 'sinks': sinks[l],
             'w_br_a': w_br_a[l], 'w_br_b': w_br_b[l], 'w_br_c': w_br_c[l], 'w_out': w_out[l],
             'router_w': router_w[l], 'router_bias': router_bias[l],
             'exp_gate': exp_gate[l], 'exp_up': exp_up[l], 'exp_down': exp_down[l],
             'shared_gate': shared_gate[l], 'shared_up': shared_up[l], 'shared_down': shared_down[l]}
        y_p, st_p = layer(y_p, c_prompt, p, None)
        y_s, st_s = layer(y_s, c_sample, p, (cache_a1_kv[l], cache_a2_kv[l], cache_a3_kv[l],
                                            cache_c_kv[l], state_ssm[l], state_conv[l]))
        new_p.append(st_p)
        new_s.append(st_s)
    a1_p, a2_p, a3_p, c_p, ssm_p, conv_p = (jnp.stack(t) for t in zip(*new_p))
    a1_s, a2_s, a3_s, c_s, ssm_s, conv_s = (jnp.stack(t) for t in zip(*new_s))
    return (y_p, y_s, a1_p, a1_s, a2_p, a2_s, a3_p, a3_s, c_p, c_s, ssm_p, ssm_s, conv_p, conv_s)
```

```python
import functools
import math

import jax
import jax.numpy as jnp
from jax import lax
from jax.experimental import pallas as pl
from jax.experimental.pallas import tpu as pltpu

F32 = jnp.float32
BF16 = jnp.bfloat16
I32 = jnp.int32

D = 2048
NP = 8192
NSEQ = 128
S = 8
NS = NSEQ * S
T = NP + NS
DEPTH = 2
HD = 64
EPS = 1e-6
A_PATTERN = ((128, 1), (512, 4), (2048, 16))
A_HEADS = 4
A_W = 256
QKV_A = 2304
C_HEADS = 12
C_KV = 3
QKV_C = 1152
SSM_INNER = 1024
SSM_HEADS = 16
SSM_N = 128
CONV_DIM = 1536
XBCDT_W = 1664
N_EXP = 64
TOP_K = 8
MOE_H = 512
ROUTED_SCALE = 2.5
BLK = 128

OFF_QKV_A = 0
OFF_QKV_C = 2304
OFF_Z = 3456
OFF_XBC = 4480
OFF_DT = 6016
OFF_GATES = 6032

TM_TOK = 256
NPT_TOK = NP // TM_TOK
BM = 256
NB = T * TOP_K // BM + N_EXP
P_ROWS = NB * BM
TT_DISP = 128


def _cp(sem, vmem_mb=48):
    return pltpu.CompilerParams(dimension_semantics=sem, vmem_limit_bytes=vmem_mb << 20)


def _sigmoid(x):
    return 1.0 / (1.0 + jnp.exp(-x))


def _silu(x):
    return x * _sigmoid(x)


def _softplus(x):
    return jnp.maximum(x, 0.0) + jnp.log1p(jnp.exp(-jnp.abs(x)))


def _rms(x):
    return x * lax.rsqrt(jnp.mean(x * x, axis=-1, keepdims=True) + EPS)


def _dot(a, b):
    return jnp.dot(a.astype(BF16), b.astype(BF16), preferred_element_type=F32)


def _dot_nt(a, b):
    return lax.dot_general(a.astype(BF16), b.astype(BF16), (((1,), (1,)), ((), ())),
                           preferred_element_type=F32)


def _dot_tn(a, b):
    return lax.dot_general(a.astype(BF16), b.astype(BF16), (((0,), (0,)), ((), ())),
                           preferred_element_type=F32)


def _sel_tile(is_prompt, p_ref, s_ref):
    return jnp.where(is_prompt, p_ref[0:1, :], s_ref[...])


def _ada_kernel(c_ref, w_ref, b_ref, o_ref):
    o_ref[...] = _dot(_silu(c_ref[...]), w_ref[...]) + b_ref[...]


def _ada(c_all, ada_w, ada_b):
    rows = c_all.shape[0]
    tn = 1024
    return pl.pallas_call(
        _ada_kernel,
        grid=(DEPTH, 6 * D // tn),
        in_specs=[pl.BlockSpec((rows, D), lambda l, j: (0, 0)),
                  pl.BlockSpec((None, D, tn), lambda l, j: (l, 0, j)),
                  pl.BlockSpec((None, 1, tn), lambda l, j: (l, 0, j))],
        out_specs=pl.BlockSpec((None, rows, tn), lambda l, j: (l, 0, j)),
        out_shape=jax.ShapeDtypeStruct((DEPTH, rows, 6 * D), F32),
        compiler_params=_cp(("arbitrary", "arbitrary")),
        name="ada",
    )(c_all, ada_w, ada_b.reshape(DEPTH, 1, 6 * D))


def _mod_specs(col):
    return [pl.BlockSpec((8, D), lambda i: (0, col)),
            pl.BlockSpec((TM_TOK, D), lambda i: (jnp.maximum(i - NPT_TOK, 0), col))]


def _prenorm_kernel(x_ref, g_ref, shp_ref, shs_ref, scp_ref, scs_ref, o_ref):
    is_p = pl.program_id(0) < NPT_TOK
    sh = _sel_tile(is_p, shp_ref, shs_ref)
    sc = _sel_tile(is_p, scp_ref, scs_ref)
    y = _rms(x_ref[...]) * g_ref[...]
    o_ref[...] = (y * (1.0 + sc) + sh).astype(o_ref.dtype)


def _prenorm(x, g, mod_p, mod_s):
    return pl.pallas_call(
        _prenorm_kernel,
        grid=(T // TM_TOK,),
        in_specs=[pl.BlockSpec((TM_TOK, D), lambda i: (i, 0)),
                  pl.BlockSpec((1, D), lambda i: (0, 0))] + _mod_specs(0) + _mod_specs(1),
        out_specs=pl.BlockSpec((TM_TOK, D), lambda i: (i, 0)),
        out_shape=jax.ShapeDtypeStruct((T, D), BF16),
        compiler_params=_cp(("parallel",)),
        name="prenorm",
    )(x, g.reshape(1, D), mod_p, mod_s, mod_p, mod_s)


def _mm_kernel(x_ref, w_ref, o_ref):
    o_ref[...] = _dot(x_ref[...], w_ref[...]).astype(o_ref.dtype)


def _matmul(x, w, tm, tn, name):
    m, k = x.shape
    n = w.shape[1]
    return pl.pallas_call(
        _mm_kernel,
        grid=(m // tm, n // tn),
        in_specs=[pl.BlockSpec((tm, k), lambda i, j: (i, 0)),
                  pl.BlockSpec((k, tn), lambda i, j: (0, j))],
        out_specs=pl.BlockSpec((tm, tn), lambda i, j: (i, j)),
        out_shape=jax.ShapeDtypeStruct((m, n), F32),
        compiler_params=_cp(("parallel", "arbitrary")),
        name=name,
    )(x, w)


def _banded_kernel(*refs, nh, k_off, v_off, with_sink, with_lse):
    refs = list(refs)
    sink_ref = refs.pop(0) if with_sink else None
    q_ref, kc_ref, kp_ref, vc_ref, vp_ref = refs[:5]
    o_ref = refs[5]
    lse_ref = refs[6] if with_lse else None
    has_prev = pl.program_id(1) > 0
    row = lax.broadcasted_iota(I32, (BLK, 2 * BLK), 0)
    col = lax.broadcasted_iota(I32, (BLK, 2 * BLK), 1)
    dist = row + BLK - col
    valid = (dist >= 0) & (dist <= BLK) & (has_prev | (col >= BLK))
    q = q_ref[...]
    kc, kp, vc, vp = kc_ref[...], kp_ref[...], vc_ref[...], vp_ref[...]
    for h in range(nh):
        qh = q[:, h * HD:(h + 1) * HD]
        ko, vo = k_off[h], v_off[h]
        kk = jnp.concatenate([kp[:, ko:ko + HD], kc[:, ko:ko + HD]], axis=0)
        vv = jnp.concatenate([vp[:, vo:vo + HD], vc[:, vo:vo + HD]], axis=0)
        s = _dot_nt(qh, kk) * (1.0 / math.sqrt(HD))
        s = jnp.where(valid, s, -jnp.inf)
        m = jnp.max(s, axis=-1, keepdims=True)
        if with_sink:
            sk = sink_ref[h]
            m = jnp.maximum(m, sk)
        e = jnp.exp(s - m)
        den = jnp.sum(e, axis=-1, keepdims=True)
        if with_sink:
            den = den + jnp.exp(sk - m)
        o = _dot(e / den, vv)
        o_ref[:, h * HD:(h + 1) * HD] = o.astype(o_ref.dtype)
        if with_lse:
            lse_ref[:, h * HD:(h + 1) * HD] = jnp.broadcast_to(m + jnp.log(den), (BLK, HD))


def _banded_a(qkv, gi, dil):
    view = qkv.reshape(T // dil, dil * QKV_A)
    nb = NP // dil // BLK
    cpr = QKV_A // A_W

    def spec(off, prev):
        if prev:
            return pl.BlockSpec((BLK, A_W), lambda r, j: (jnp.maximum(j - 1, 0), r * cpr + off + gi))
        return pl.BlockSpec((BLK, A_W), lambda r, j: (j, r * cpr + off + gi))

    offs = tuple(h * HD for h in range(A_HEADS))
    out_spec = pl.BlockSpec((BLK, A_W), lambda r, j: (j, r))
    o, lse = pl.pallas_call(
        functools.partial(_banded_kernel, nh=A_HEADS, k_off=offs, v_off=offs,
                          with_sink=False, with_lse=True),
        grid=(dil, nb),
        in_specs=[spec(0, False), spec(3, False), spec(3, True), spec(6, False), spec(6, True)],
        out_specs=[out_spec, out_spec],
        out_shape=[jax.ShapeDtypeStruct((NP // dil, dil * A_W), F32)] * 2,
        compiler_params=_cp(("parallel", "arbitrary")),
        name=f"banded_a{gi}",
    )(view, view, view, view, view)
    return o.reshape(NP, A_W), lse.reshape(NP, A_W)


def _banded_c(qkv, sinks):
    nq = C_HEADS * HD
    kvw = 2 * C_KV * HD
    k_off = tuple((h // 4) * HD for h in range(C_HEADS))
    v_off = tuple(C_KV * HD + (h // 4) * HD for h in range(C_HEADS))
    cur = pl.BlockSpec((BLK, kvw), lambda r, j: (j, nq // kvw))
    prev = pl.BlockSpec((BLK, kvw), lambda r, j: (jnp.maximum(j - 1, 0), nq // kvw))
    return pl.pallas_call(
        functools.partial(_banded_kernel, nh=C_HEADS, k_off=k_off, v_off=v_off,
                          with_sink=True, with_lse=False),
        grid=(1, NP // BLK),
        in_specs=[pl.BlockSpec(memory_space=pltpu.SMEM),
                  pl.BlockSpec((BLK, nq), lambda r, j: (j, 0)), cur, prev, cur, prev],
        out_specs=pl.BlockSpec((BLK, nq), lambda r, j: (j, 0)),
        out_shape=jax.ShapeDtypeStruct((NP, nq), BF16),
        compiler_params=_cp(("parallel", "arbitrary")),
        name="banded_c",
    )(sinks, qkv, qkv, qkv, qkv, qkv)


def _combine_kernel(o1, o2, o3, l1, l2, l3, out_ref):
    a, b, c = l1[...], l2[...], l3[...]
    m = jnp.maximum(jnp.maximum(a, b), c)
    ea, eb, ec = jnp.exp(a - m), jnp.exp(b - m), jnp.exp(c - m)
    tot = ea + eb + ec
    out_ref[...] = ((ea / tot) * o1[...] + (eb / tot) * o2[...] + (ec / tot) * o3[...]).astype(out_ref.dtype)


def _combine_groups(outs, lses):
    tm = 512
    spec = pl.BlockSpec((tm, A_W), lambda i: (i, 0))
    return pl.pallas_call(
        _combine_kernel,
        grid=(NP // tm,),
        in_specs=[spec] * 6,
        out_specs=spec,
        out_shape=jax.ShapeDtypeStruct((NP, A_W), BF16),
        compiler_params=_cp(("parallel",)),
        name="combine_groups",
    )(*outs, *lses)


def _dec_scores(q, nq, nk, cache_ref, knew, dil):
    rep = nq // nk
    ncol = nq * S
    kw = nk * HD
    qrows = jnp.concatenate([q[:, h * HD:(h + 1) * HD] for h in range(nq)], axis=0)
    qt = jnp.concatenate([qrows] * nk, axis=1) if nk > 1 else qrows
    r_i = lax.broadcasted_iota(I32, (ncol, kw), 0)
    l_i = lax.broadcasted_iota(I32, (ncol, kw), 1)
    qbd = jnp.where((r_i // (S * rep)) == (l_i // HD), qt, 0.0)
    scale = 1.0 / math.sqrt(HD)
    nres = min(dil, S)
    pieces = []
    for r in range(nres):
        kr = cache_ref[:, r * 2 * kw:r * 2 * kw + kw]
        st = _dot_nt(kr, qbd) * scale
        n_i = lax.broadcasted_iota(I32, (BLK, ncol), 0)
        s_i = lax.broadcasted_iota(I32, (BLK, ncol), 1) % S
        ok = ((s_i % dil) == r) & (n_i >= s_i // dil)
        pieces.append((jnp.where(ok, st, -jnp.inf), ok))
    st = _dot_nt(knew, qbd) * scale
    sp_i = lax.broadcasted_iota(I32, (S, ncol), 0)
    s_i = lax.broadcasted_iota(I32, (S, ncol), 1) % S
    ok = (sp_i <= s_i) & (((s_i - sp_i) % dil) == 0)
    pieces.append((jnp.where(ok, st, -jnp.inf), ok))
    return pieces


def _dec_softmax(pieces, sink_row):
    m = functools.reduce(jnp.maximum, [jnp.max(p, axis=0, keepdims=True) for p, _ in pieces])
    if sink_row is not None:
        m = jnp.maximum(m, sink_row)
    es = [jnp.exp(p - m) for p, _ in pieces]
    den = functools.reduce(jnp.add, [jnp.sum(e, axis=0, keepdims=True) for e in es])
    if sink_row is not None:
        den = den + jnp.exp(sink_row - m)
    return es, m, den


def _dec_pv(ws, nk, cache_ref, vnew, dil):
    kw = nk * HD
    acc = None
    for r, w in enumerate(ws[:-1]):
        vr = cache_ref[:, r * 2 * kw + kw:(r + 1) * 2 * kw]
        t = _dot_tn(w, vr)
        acc = t if acc is None else acc + t
    return acc + _dot_tn(ws[-1], vnew)


def _pick_heads(o_all, nq, nk):
    rep = nq // nk
    cols = []
    for h in range(nq):
        kh = h // rep
        cols.append(o_all[h * S:(h + 1) * S, kh * HD:(kh + 1) * HD])
    return jnp.concatenate(cols, axis=1)


def _shift_cache(new_ref, old_ref, knew, vnew, dil):
    pw = 2 * A_W
    if S % dil == 0:
        rs = S // dil
        new_ref[0:BLK - rs, :] = old_ref[rs:BLK, :]
        first_row, first_res = BLK - rs, 0
    else:
        half = (dil - S) * pw
        new_ref[:, 0:half] = old_ref[:, S * pw:dil * pw]
        new_ref[0:BLK - 1, half:dil * pw] = old_ref[1:BLK, 0:S * pw]
        first_row, first_res = BLK - 1, dil - S
    if dil == 1:
        new_ref[first_row:BLK, 0:A_W] = knew
        new_ref[first_row:BLK, A_W:pw] = vnew
        return
    for s in range(S):
        row = first_row + (first_res + s) // dil
        c0 = ((first_res + s) % dil) * pw
        new_ref[row:row + 1, c0:c0 + A_W] = knew[s:s + 1, :]
        new_ref[row:row + 1, c0 + A_W:c0 + pw] = vnew[s:s + 1, :]


def _decode_attn_kernel(sink_ref, qa_ref, qc_ref, c1_ref, c2_ref, c3_ref, cc_ref, *rest):
    n_alias = len(rest) - 6
    n1_ref, n2_ref, n3_ref, nc_ref, oa_ref, oc_ref = rest[n_alias:]
    qa = qa_ref[...]
    qc = qc_ref[...]
    caches = (c1_ref, c2_ref, c3_ref)
    news = (n1_ref, n2_ref, n3_ref)
    grp = []
    for gi, (w, dil) in enumerate(A_PATTERN):
        q = qa[:, gi * A_W:(gi + 1) * A_W]
        knew = qa[:, 768 + gi * A_W:768 + (gi + 1) * A_W]
        vnew = qa[:, 1536 + gi * A_W:1536 + (gi + 1) * A_W]
        pieces = _dec_scores(q, A_HEADS, A_HEADS, caches[gi], knew, dil)
        es, m, den = _dec_softmax(pieces, None)
        grp.append((es, m + jnp.log(den), den, vnew, dil))
        _shift_cache(news[gi], caches[gi], knew, vnew, dil)
    lmax = functools.reduce(jnp.maximum, [g[1] for g in grp])
    gws = [jnp.exp(g[1] - lmax) for g in grp]
    gtot = functools.reduce(jnp.add, gws)
    o_all = None
    for gi, (es, _, den, vnew, dil) in enumerate(grp):
        wrow = (gws[gi] / gtot) / den
        t = _dec_pv([e * wrow for e in es], A_HEADS, caches[gi], vnew, dil)
        o_all = t if o_all is None else o_all + t
    oa_ref[...] = _pick_heads(o_all, A_HEADS, A_HEADS).astype(oa_ref.dtype)
    nqc = C_HEADS * HD
    kvw = C_KV * HD
    q = qc[:, 0:nqc]
    knew = qc[:, nqc:nqc + kvw]
    vnew = qc[:, nqc + kvw:nqc + 2 * kvw]
    pieces = _dec_scores(q, C_HEADS, C_KV, cc_ref, knew, 1)
    es, m, den = _dec_softmax(pieces, sink_ref[...])
    o_c = _dec_pv([e / den for e in es], C_KV, cc_ref, vnew, 1)
    oc_ref[...] = _pick_heads(o_c, C_HEADS, C_KV).astype(oc_ref.dtype)
    nc_ref[0:BLK - S, :] = cc_ref[S:BLK, :]
    nc_ref[BLK - S:BLK, :] = qc[:, nqc:nqc + 2 * kvw]


def _decode_attn(layer, sink_row, qkv_a, qkv_c, caches, prev_out):
    row0 = NP // S

    def cache_spec(c):
        return pl.BlockSpec((None, None) + c.shape[2:], lambda b: (layer, b, 0, 0))

    in_specs = [pl.BlockSpec((1, C_HEADS * S), lambda b: (0, 0)),
                pl.BlockSpec((S, QKV_A), lambda b: (row0 + b, 0)),
                pl.BlockSpec((S, QKV_C), lambda b: (row0 + b, 0))] + [cache_spec(c) for c in caches]
    args = [sink_row, qkv_a, qkv_c] + list(caches)
    aliases = {}
    if prev_out is not None:
        for k, p in enumerate(prev_out):
            in_specs.append(pl.BlockSpec(memory_space=pl.ANY))
            args.append(p)
            aliases[len(args) - 1] = k
    out_specs = [cache_spec(c) for c in caches] + [
        pl.BlockSpec((S, A_W), lambda b: (b, 0)),
        pl.BlockSpec((S, C_HEADS * HD), lambda b: (b, 0))]
    out_shape = [jax.ShapeDtypeStruct(c.shape, F32) for c in caches] + [
        jax.ShapeDtypeStruct((NS, A_W), BF16), jax.ShapeDtypeStruct((NS, C_HEADS * HD), BF16)]
    return pl.pallas_call(
        _decode_attn_kernel,
        grid=(NSEQ,),
        in_specs=in_specs,
        out_specs=out_specs,
        out_shape=out_shape,
        input_output_aliases=aliases,
        compiler_params=_cp(("arbitrary",), 56),
        name=f"decode_attn{layer}",
    )(*args)


def _ssd_kernel(xbcdt_ref, z_ref, conv0_ref, h0_ref, convw_ref, convb_ref, dtb_ref, arow_ref,
                dskip_ref, nrm_ref, y_ref, hout_ref, xp_ref, h_ref, yacc_ref, *, q_len):
    c = pl.program_id(1)
    Q = q_len

    @pl.when(c == 0)
    def _():
        xp_ref[0:8, :] = conv0_ref[...]
        h_ref[...] = h0_ref[...]

    xbc = xbcdt_ref[:, 0:CONV_DIM]
    xp_ref[8:8 + Q, :] = xbc
    w = convw_ref[...]
    acc = convb_ref[...] + w[3:4, :] * xbc
    acc = acc + w[2:3, :] * xp_ref[7:7 + Q, :]
    acc = acc + w[1:2, :] * xp_ref[6:6 + Q, :]
    acc = acc + w[0:1, :] * xp_ref[5:5 + Q, :]
    u = _silu(acc)
    xp_ref[0:8, :] = xbc[Q - 8:Q, :]

    dtp = _softplus(xbcdt_ref[:, CONV_DIM:XBCDT_W] + dtb_ref[...])
    da = dtp * arow_ref[...]
    ri = lax.broadcasted_iota(I32, (Q, Q), 0)
    ci = lax.broadcasted_iota(I32, (Q, Q), 1)
    tri = ri >= ci
    acs = jnp.dot(tri.astype(F32), da, precision=lax.Precision.HIGHEST, preferred_element_type=F32)
    acs_t = acs.T
    dt_t = dtp.T
    hpg = SSM_HEADS // 2
    for g in range(2):
        bg = u[:, SSM_INNER + g * SSM_N:SSM_INNER + (g + 1) * SSM_N]
        cg = u[:, SSM_INNER + (2 + g) * SSM_N:SSM_INNER + (3 + g) * SSM_N]
        cb = _dot_nt(cg, bg)
        for hh in range(hpg):
            hd = g * hpg + hh
            a_col = acs[:, hd:hd + 1]
            seg = a_col - acs_t[hd:hd + 1, :]
            decay = jnp.exp(jnp.where(tri, seg, -jnp.inf))
            mh = cb * decay * dt_t[hd:hd + 1, :]
            xh = u[:, hd * HD:(hd + 1) * HD]
            hst = h_ref[hd]
            y = _dot(mh, xh) + _dot_nt(cg, hst) * jnp.exp(a_col)
            a_last = acs[Q - 1:Q, hd:hd + 1]
            tail = jnp.exp(a_last - a_col) * dtp[:, hd:hd + 1]
            h_ref[hd] = hst * jnp.exp(a_last) + _dot_tn(xh * tail, bg)
            yacc_ref[:, hd * HD:(hd + 1) * HD] = y + dskip_ref[:, hd * HD:(hd + 1) * HD] * xh
    yz = yacc_ref[...] * _silu(z_ref[...])
    half = SSM_INNER // 2
    for g in range(2):
        yg = yz[:, g * half:(g + 1) * half]
        y_ref[:, g * half:(g + 1) * half] = (_rms(yg) * nrm_ref[:, g * half:(g + 1) * half]).astype(y_ref.dtype)

    @pl.when(c == pl.num_programs(1) - 1)
    def _():
        hout_ref[...] = h_ref[...]


def _ssd(layer, xbcdt, z, conv0, h0, h0_layer, convw, convb, dtb, arow, dskip, nrm, nseq, q_len, row0):
    nchunks = (NP if nseq == 1 else S) // q_len
    rows = nseq * nchunks * q_len
    vec = lambda n: pl.BlockSpec((1, n), lambda b, c: (0, 0))
    return pl.pallas_call(
        functools.partial(_ssd_kernel, q_len=q_len),
        grid=(nseq, nchunks),
        in_specs=[pl.BlockSpec((q_len, XBCDT_W), lambda b, c: (row0 + b * nchunks + c, 0)),
                  pl.BlockSpec((q_len, SSM_INNER), lambda b, c: (row0 + b * nchunks + c, 0)),
                  pl.BlockSpec((None, 8, CONV_DIM), lambda b, c: (b, 0, 0)),
                  pl.BlockSpec((None, None, SSM_HEADS, HD, SSM_N), lambda b, c: (h0_layer, b, 0, 0, 0)),
                  pl.BlockSpec((4, CONV_DIM), lambda b, c: (0, 0)),
                  vec(CONV_DIM), vec(128), vec(128), vec(SSM_INNER), vec(SSM_INNER)],
        out_specs=[pl.BlockSpec((q_len, SSM_INNER), lambda b, c: (b * nchunks + c, 0)),
                   pl.BlockSpec((None, SSM_HEADS, HD, SSM_N), lambda b, c: (b, 0, 0, 0))],
        out_shape=[jax.ShapeDtypeStruct((rows, SSM_INNER), BF16),
                   jax.ShapeDtypeStruct((nseq, SSM_HEADS, HD, SSM_N), F32)],
        scratch_shapes=[pltpu.VMEM((q_len + 8, CONV_DIM), F32),
                        pltpu.VMEM((SSM_HEADS, HD, SSM_N), F32),
                        pltpu.VMEM((q_len, SSM_INNER), F32)],
        compiler_params=_cp(("arbitrary", "arbitrary")),
        name=f"ssd_q{q_len}",
    )(xbcdt, z, conv0, h0, convw, convb, dtb, arow, dskip, nrm)


def _merge_kernel(h_ref, oa_ref, ob_ref, oc_ref, g0_ref, g1_ref, g2_ref, wa_ref, wb_ref, wc_ref, o_ref):
    h = h_ref[...]
    acc = _sigmoid(_dot(h, g0_ref[...])) * _dot(oa_ref[...], wa_ref[...])
    acc = acc + _sigmoid(_dot(h, g1_ref[...])) * _dot(ob_ref[...], wb_ref[...])
    acc = acc + _sigmoid(_dot(h, g2_ref[...])) * _dot(oc_ref[...], wc_ref[...])
    o_ref[...] = acc.astype(o_ref.dtype)


def _merge(layer, h1, oa, ob, oc, wgate, w_br_a, w_br_b, w_br_c):
    tm, tn = 1024, 512
    nj = D // tn
    row = lambda k: pl.BlockSpec((tm, k), lambda i, j: (i, 0))
    gate = lambda b: pl.BlockSpec((D, tn), lambda i, j: (0, b * nj + j))
    br = lambda k: pl.BlockSpec((None, k, tn), lambda i, j: (layer, 0, j))
    return pl.pallas_call(
        _merge_kernel,
        grid=(T // tm, nj),
        in_specs=[row(D), row(A_W), row(SSM_INNER), row(C_HEADS * HD), gate(0), gate(1), gate(2),
                  br(A_W), br(SSM_INNER), br(C_HEADS * HD)],
        out_specs=pl.BlockSpec((tm, tn), lambda i, j: (i, j)),
        out_shape=jax.ShapeDtypeStruct((T, D), BF16),
        compiler_params=_cp(("parallel", "arbitrary"), 56),
        name="merge",
    )(h1, oa, ob, oc, wgate, wgate, wgate, w_br_a, w_br_b, w_br_c)


def _outproj_kernel(m_ref, w_ref, x_ref, gpost_ref, gpre_ref, gap_ref, gas_ref, shp_ref, shs_ref,
                    scp_ref, scs_ref, x_out_ref, hf_ref):
    is_p = pl.program_id(0) < NPT_TOK
    ga = _sel_tile(is_p, gap_ref, gas_ref)
    sh = _sel_tile(is_p, shp_ref, shs_ref)
    sc = _sel_tile(is_p, scp_ref, scs_ref)
    mix = _dot(m_ref[...], w_ref[...])
    xn = x_ref[...] + ga * (_rms(mix) * gpost_ref[...])
    x_out_ref[...] = xn
    hf_ref[...] = (_rms(xn) * gpre_ref[...]) * (1.0 + sc) + sh


def _outproj(merged, w_out_bf, x, g_post, g_pre, mod_p, mod_s):
    tok = pl.BlockSpec((TM_TOK, D), lambda i: (i, 0))
    vec = pl.BlockSpec((1, D), lambda i: (0, 0))
    return pl.pallas_call(
        _outproj_kernel,
        grid=(T // TM_TOK,),
        in_specs=[tok, pl.BlockSpec((D, D), lambda i: (0, 0)), tok, vec, vec]
        + _mod_specs(2) + _mod_specs(3) + _mod_specs(4),
        out_specs=[tok, tok],
        out_shape=[jax.ShapeDtypeStruct((T, D), F32)] * 2,
        compiler_params=_cp(("parallel",), 56),
        name="outproj",
    )(merged, w_out_bf, x, g_post.reshape(1, D), g_pre.reshape(1, D), mod_p, mod_s, mod_p, mod_s, mod_p, mod_s)


def _router_kernel(hf_ref, rwt_ref, bias_ref, topi_ref, topw_ref):
    tm = hf_ref.shape[0]
    logits = lax.dot_general(rwt_ref[...], hf_ref[...], (((1,), (1,)), ((), ())),
                             precision=lax.Precision.HIGHEST, preferred_element_type=F32)
    scores = _sigmoid(logits)
    sel = scores + bias_ref[...]
    ng = 8
    sel3 = sel.reshape(ng, N_EXP // ng, tm)
    idx3 = lax.broadcasted_iota(I32, sel3.shape, 1)
    m1 = jnp.max(sel3, axis=1, keepdims=True)
    first = jnp.min(jnp.where(sel3 == m1, idx3, N_EXP), axis=1, keepdims=True)
    m2 = jnp.max(jnp.where(idx3 == first, -jnp.inf, sel3), axis=1, keepdims=True)
    gs = m1 + m2
    gidx = lax.broadcasted_iota(I32, (ng, 1, tm), 0)
    gmask = jnp.zeros((ng, 1, tm), jnp.bool_)
    for _ in range(4):
        m = jnp.max(gs, axis=0, keepdims=True)
        f = jnp.min(jnp.where(gs == m, gidx, ng), axis=0, keepdims=True)
        hit = gidx == f
        gmask = jnp.logical_or(gmask, hit)
        gs = jnp.where(hit, -jnp.inf, gs)
    masked = jnp.where(gmask, sel3, -jnp.inf).reshape(N_EXP, tm)
    eidx = lax.broadcasted_iota(I32, (N_EXP, tm), 0)
    tis, tws = [], []
    for _ in range(TOP_K):
        m = jnp.max(masked, axis=0, keepdims=True)
        f = jnp.min(jnp.where(masked == m, eidx, N_EXP), axis=0, keepdims=True)
        hit = eidx == f
        tis.append(f)
        tws.append(jnp.sum(jnp.where(hit, scores, 0.0), axis=0, keepdims=True))
        masked = jnp.where(hit, -jnp.inf, masked)
    wsum = functools.reduce(jnp.add, tws)
    topi_ref[...] = jnp.concatenate(tis, axis=0)
    topw_ref[...] = jnp.concatenate([w / wsum * ROUTED_SCALE for w in tws], axis=0)


def _router(hf, rwt, bias_col):
    tm = 512
    return pl.pallas_call(
        _router_kernel,
        grid=(T // tm,),
        in_specs=[pl.BlockSpec((tm, D), lambda i: (i, 0)),
                  pl.BlockSpec((N_EXP, D), lambda i: (0, 0)),
                  pl.BlockSpec((N_EXP, 1), lambda i: (0, 0))],
        out_specs=[pl.BlockSpec((TOP_K, tm), lambda i: (0, i))] * 2,
        out_shape=[jax.ShapeDtypeStruct((TOP_K, T), I32), jax.ShapeDtypeStruct((TOP_K, T), F32)],
        compiler_params=_cp(("parallel",)),
        name="router",
    )(hf, rwt, bias_col)


def _route_tables(topi):
    onehot = topi[None] == jnp.arange(N_EXP, dtype=I32)[:, None, None]
    tok = jnp.any(onehot, axis=1).astype(I32)
    incl = jnp.cumsum(tok, axis=1)
    counts = incl[:, -1]
    padded = (counts + BM - 1) // BM * BM
    pad_end = jnp.cumsum(padded)
    base = (pad_end - padded)[:, None] + incl - tok
    dest = jnp.sum(jnp.where(onehot, base[:, None, :], 0), axis=0).astype(I32)
    starts = jnp.arange(NB, dtype=I32) * BM
    block_e = jnp.minimum(jnp.sum(pad_end[None, :] <= starts[:, None], axis=1), N_EXP - 1).astype(I32)
    n_used = (pad_end[-1] // BM).astype(I32).reshape(1)
    return dest, block_e, n_used


def _dispatch_kernel(dest_ref, hf_ref, xs_in_ref, xs_ref, sem):
    del xs_in_ref
    base = pl.program_id(0) * TT_DISP

    def body(t, carry):
        for k in range(TOP_K):
            pltpu.make_async_copy(hf_ref.at[pl.ds(base + t, 1)], xs_ref.at[pl.ds(dest_ref[k, t], 1)], sem).start()
        return carry

    lax.fori_loop(0, TT_DISP, body, 0)
    for k in range(TOP_K):
        pltpu.make_async_copy(hf_ref.at[pl.ds(0, TT_DISP)], xs_ref.at[pl.ds(0, TT_DISP)], sem).wait()


def _dispatch(dest, hf, xs_buf):
    return pl.pallas_call(
        _dispatch_kernel,
        grid=(T // TT_DISP,),
        in_specs=[pl.BlockSpec((TOP_K, TT_DISP), lambda i: (0, i), memory_space=pltpu.SMEM),
                  pl.BlockSpec(memory_space=pl.ANY),
                  pl.BlockSpec(memory_space=pl.ANY)],
        out_specs=pl.BlockSpec(memory_space=pl.ANY),
        out_shape=jax.ShapeDtypeStruct((P_ROWS, D), F32),
        scratch_shapes=[pltpu.SemaphoreType.DMA(())],
        input_output_aliases={2: 0},
        compiler_params=_cp(("arbitrary",)),
        name="moe_dispatch",
    )(dest, hf, xs_buf)


def _swiglu(x, wg, wu, wd):
    return _dot(_silu(_dot(x, wg)) * _dot(x, wu), wd)


def _expert_kernel(be_ref, nu_ref, x_ref, wg_ref, wu_ref, wd_ref, o_ref):
    del be_ref
    used = pl.program_id(0) < nu_ref[0]

    @pl.when(used)
    def _():
        o_ref[...] = _swiglu(x_ref[...], wg_ref[...], wu_ref[...], wd_ref[...])

    @pl.when(jnp.logical_not(used))
    def _():
        o_ref[...] = jnp.zeros_like(o_ref)


def _experts(layer, xs, block_e, n_used, exp_gate, exp_up, exp_down):
    wspec = lambda a, b: pl.BlockSpec((None, None, a, b), lambda i, be, nu: (layer, be[i], 0, 0))
    return pl.pallas_call(
        _expert_kernel,
        grid_spec=pltpu.PrefetchScalarGridSpec(
            num_scalar_prefetch=2,
            grid=(NB,),
            in_specs=[pl.BlockSpec((BM, D), lambda i, be, nu: (i, 0)),
                      wspec(D, MOE_H), wspec(D, MOE_H), wspec(MOE_H, D)],
            out_specs=pl.BlockSpec((BM, D), lambda i, be, nu: (i, 0))),
        out_shape=jax.ShapeDtypeStruct((P_ROWS, D), F32),
        compiler_params=_cp(("arbitrary",), 56),
        name="moe_experts",
    )(block_e, n_used, xs, exp_gate, exp_up, exp_down)


def _shared_kernel(x_ref, wg_ref, wu_ref, wd_ref, o_ref):
    o_ref[...] = _swiglu(x_ref[...], wg_ref[...], wu_ref[...], wd_ref[...])


def _shared_expert(layer, hf, wg, wu, wd):
    tm = 512
    wspec = lambda a, b: pl.BlockSpec((None, a, b), lambda i: (layer, 0, 0))
    return pl.pallas_call(
        _shared_kernel,
        grid=(T // tm,),
        in_specs=[pl.BlockSpec((tm, D), lambda i: (i, 0)), wspec(D, MOE_H), wspec(D, MOE_H), wspec(MOE_H, D)],
        out_specs=pl.BlockSpec((tm, D), lambda i: (i, 0)),
        out_shape=jax.ShapeDtypeStruct((T, D), F32),
        compiler_params=_cp(("parallel",), 56),
        name="moe_shared",
    )(hf, wg, wu, wd)


def _moe_combine_kernel(dest_ref, ys_ref, w_ref, fsh_ref, x_ref, gpost_ref, gap_ref, gas_ref,
                        o_ref, gbuf, sem):
    tt = TT_DISP
    is_p = pl.program_id(0) < NP // tt

    def body(t, carry):
        for k in range(TOP_K):
            pltpu.make_async_copy(ys_ref.at[pl.ds(dest_ref[k, t], 1)], gbuf.at[k, pl.ds(t, 1)], sem).start()
        return carry

    lax.fori_loop(0, tt, body, 0)
    for k in range(TOP_K):
        pltpu.make_async_copy(ys_ref.at[pl.ds(0, tt)], gbuf.at[k], sem).wait()
    w = w_ref[...]
    f = fsh_ref[...]
    for k in range(TOP_K):
        f = f + gbuf[k] * w[:, k:k + 1]
    ga = jnp.where(is_p, gap_ref[0:1, :], gas_ref[...])
    o_ref[...] = x_ref[...] + ga * (_rms(f) * gpost_ref[...])


def _moe_combine(dest, ys, topw_t, f_shared, x, g_post, mod_p, mod_s):
    tt = TT_DISP
    tok = pl.BlockSpec((tt, D), lambda i: (i, 0))
    return pl.pallas_call(
        _moe_combine_kernel,
        grid=(T // tt,),
        in_specs=[pl.BlockSpec((TOP_K, tt), lambda i: (0, i), memory_space=pltpu.SMEM),
                  pl.BlockSpec(memory_space=pl.ANY),
                  pl.BlockSpec((tt, TOP_K), lambda i: (i, 0)),
                  tok, tok, pl.BlockSpec((1, D), lambda i: (0, 0)),
                  pl.BlockSpec((8, D), lambda i: (0, 5)),
                  pl.BlockSpec((tt, D), lambda i: (jnp.maximum(i - NP // tt, 0), 5))],
        out_specs=tok,
        out_shape=jax.ShapeDtypeStruct((T, D), F32),
        scratch_shapes=[pltpu.VMEM((TOP_K, tt, D), F32), pltpu.SemaphoreType.DMA(())],
        compiler_params=_cp(("arbitrary",)),
        name="moe_combine",
    )(dest, ys, topw_t, f_shared, x, g_post.reshape(1, D), mod_p, mod_s)


def _pad_cols(w, n):
    return jnp.pad(w, ((0, 0), (0, n - w.shape[1])))


def _layer(l, x, mod_p, mod_s, p, dec_caches, dec_prev, xs_buf):
    w_in = p['w_in'][l]
    h1 = _prenorm(x, p['norm_pre_mix'][l], mod_p, mod_s)
    qkv_a = _matmul(h1, w_in[:, OFF_QKV_A:OFF_QKV_C].astype(BF16), 1024, 768, "proj_qkv_a")
    qkv_c = _matmul(h1, w_in[:, OFF_QKV_C:OFF_Z].astype(BF16), 1024, QKV_C, "proj_qkv_c")
    z = _matmul(h1, w_in[:, OFF_Z:OFF_XBC].astype(BF16), 1024, SSM_INNER, "proj_z")
    xbcdt = _matmul(h1, _pad_cols(w_in[:, OFF_XBC:OFF_GATES], XBCDT_W).astype(BF16), 1024, XBCDT_W, "proj_xbcdt")

    outs, lses = zip(*[_banded_a(qkv_a, gi, dil) for gi, (_, dil) in enumerate(A_PATTERN)])
    oa_p = _combine_groups(outs, lses)
    oc_p = _banded_c(qkv_c, p['sinks'][l])
    sink_row = jnp.repeat(p['sinks'][l], S).reshape(1, C_HEADS * S)
    n1, n2, n3, ncc, oa_s, oc_s = _decode_attn(l, sink_row, qkv_a, qkv_c, dec_caches, dec_prev)

    dtb = jnp.pad(p['dt_bias'][l], (0, 128 - SSM_HEADS)).reshape(1, 128)
    arow = jnp.pad(-jnp.exp(p['a_log'][l]), (0, 128 - SSM_HEADS)).reshape(1, 128)
    dskip = jnp.repeat(p['d_skip'][l], HD).reshape(1, SSM_INNER)
    nrm = p['ssm_norm'][l].reshape(1, SSM_INNER)
    convw = p['conv_w'][l]
    convb = p['conv_b'][l].reshape(1, CONV_DIM)
    ob_p, ssm_p = _ssd(l, xbcdt, z, jnp.zeros((1, 8, CONV_DIM), F32),
                       jnp.zeros((1, 1, SSM_HEADS, HD, SSM_N), F32), 0,
                       convw, convb, dtb, arow, dskip, nrm, 1, BLK, 0)
    conv0_s = jnp.pad(p['state_conv'][l], ((0, 0), (5, 0), (0, 0)))
    ob_s, ssm_s = _ssd(l, xbcdt, z, conv0_s, p['state_ssm'], l,
                       convw, convb, dtb, arow, dskip, nrm, NSEQ, S, NP // S)

    oa = jnp.concatenate([oa_p, oa_s], axis=0)
    ob = jnp.concatenate([ob_p, ob_s], axis=0)
    oc = jnp.concatenate([oc_p, oc_s], axis=0)
    merged = _merge(l, h1, oa, ob, oc, w_in[:, OFF_GATES:].astype(BF16), p['w_br_a'], p['w_br_b'], p['w_br_c'])
    x, hf = _outproj(merged, p['w_out'][l].astype(BF16), x, p['norm_post_mix'][l], p['norm_pre_ffn'][l],
                     mod_p, mod_s)

    topi, topw = _router(hf, p['router_w'][l].T, p['router_bias'][l].reshape(N_EXP, 1))
    dest, block_e, n_used = _route_tables(topi)
    xs = _dispatch(dest, hf, xs_buf)
    ys = _experts(l, xs, block_e, n_used, p['exp_gate'], p['exp_up'], p['exp_down'])
    f_shared = _shared_expert(l, hf, p['shared_gate'], p['shared_up'], p['shared_down'])
    x = _moe_combine(dest, ys, topw.T, f_shared, x, p['norm_post_ffn'][l], mod_p, mod_s)

    def last_kv(src, k_off, v_off, rows, heads):
        k = src[NP - rows:NP, k_off:k_off + heads * HD].reshape(rows, heads, HD)
        v = src[NP - rows:NP, v_off:v_off + heads * HD].reshape(rows, heads, HD)
        return jnp.stack([k, v], axis=1)[None]

    a_states = [last_kv(qkv_a, 768 + gi * A_W, 1536 + gi * A_W, w, A_HEADS)
                for gi, (w, _) in enumerate(A_PATTERN)]
    c_state = last_kv(qkv_c, C_HEADS * HD, C_HEADS * HD + C_KV * HD, BLK, C_KV)
    conv_p = xbcdt[NP - 3:NP, :CONV_DIM][None]
    conv_s = xbcdt[NP:, :CONV_DIM].reshape(NSEQ, S, CONV_DIM)[:, S - 3:]
    prompt_state = (*a_states, c_state, ssm_p.reshape(1, 2, SSM_HEADS // 2, HD, SSM_N), conv_p)
    return x, prompt_state, (n1, n2, n3, ncc), ssm_s, conv_s, xs


def kernel(x_prompt, x_sample, cache_a1_kv, cache_a2_kv, cache_a3_kv, cache_c_kv, state_ssm, state_conv,
           c_prompt, c_sample, ada_w, ada_b, norm_pre_mix, norm_post_mix, norm_pre_ffn, norm_post_ffn,
           w_in, conv_w, conv_b, dt_bias, a_log, d_skip, ssm_norm, sinks, w_br_a, w_br_b, w_br_c, w_out,
           router_w, router_bias, exp_gate, exp_up, exp_down, shared_gate, shared_up, shared_down):
    p = dict(norm_pre_mix=norm_pre_mix, norm_post_mix=norm_post_mix, norm_pre_ffn=norm_pre_ffn,
             norm_post_ffn=norm_post_ffn, w_in=w_in, conv_w=conv_w, conv_b=conv_b, dt_bias=dt_bias,
             a_log=a_log, d_skip=d_skip, ssm_norm=ssm_norm, sinks=sinks, w_br_a=w_br_a, w_br_b=w_br_b,
             w_br_c=w_br_c, w_out=w_out, router_w=router_w, router_bias=router_bias, exp_gate=exp_gate,
             exp_up=exp_up, exp_down=exp_down, shared_gate=shared_gate, shared_up=shared_up,
             shared_down=shared_down, state_conv=state_conv,
             state_ssm=state_ssm.reshape(DEPTH, NSEQ, SSM_HEADS, HD, SSM_N))
    x = jnp.concatenate([x_prompt.reshape(NP, D), x_sample.reshape(NS, D)], axis=0)
    c_all = jnp.concatenate([c_prompt, c_sample, jnp.zeros((7, D), F32)], axis=0)
    mod = _ada(c_all, ada_w, ada_b)
    dec_caches = tuple(c.reshape(DEPTH, NSEQ, BLK, dil * 2 * A_W) for c, (_, dil) in
                       zip((cache_a1_kv, cache_a2_kv, cache_a3_kv), A_PATTERN))
    dec_caches += (cache_c_kv.reshape(DEPTH, NSEQ, BLK, 2 * C_KV * HD),)
    xs_buf = jnp.zeros((P_ROWS, D), F32)
    dec_prev = None
    prompt_states, ssm_s, conv_s = [], [], []
    for l in range(DEPTH):
        mod_s = jnp.repeat(mod[l, 1:1 + NSEQ], S, axis=0)
        x, pst, dec_prev, ssm_l, conv_l, xs_buf = _layer(l, x, mod[l], mod_s, p, dec_caches, dec_prev, xs_buf)
        prompt_states.append(pst)
        ssm_s.append(ssm_l)
        conv_s.append(conv_l)
    a1_p, a2_p, a3_p, c_p, ssm_p, conv_p = (jnp.stack(t) for t in zip(*prompt_states))
    n1, n2, n3, ncc = dec_prev
    y_p = x[:NP].reshape(1, NP, D)
    y_s = x[NP:].reshape(NSEQ, S, D)
    return (y_p, y_s,
            a1_p, n1.reshape(cache_a1_kv.shape), a2_p, n2.reshape(cache_a2_kv.shape),
            a3_p, n3.reshape(cache_a3_kv.shape), c_p, ncc.reshape(cache_c_kv.shape),
            ssm_p, jnp.stack(ssm_s).reshape(state_ssm.shape), conv_p, jnp.stack(conv_s))
```

```python
import functools
import math

import jax
import jax.numpy as jnp
from jax import lax
from jax.experimental import pallas as pl
from jax.experimental.pallas import tpu as pltpu

F32 = jnp.float32
BF16 = jnp.bfloat16
I32 = jnp.int32

D = 2048
NP = 8192
NSEQ = 128
S = 8
NS = NSEQ * S
T = NP + NS
DEPTH = 2
HD = 64
EPS = 1e-6
A_PATTERN = ((128, 1), (512, 4), (2048, 16))
A_HEADS = 4
A_W = 256
QKV_A = 2304
C_HEADS = 12
C_KV = 3
QKV_C = 1152
SSM_INNER = 1024
SSM_HEADS = 16
SSM_N = 128
CONV_DIM = 1536
XBCDT_W = 1664
N_EXP = 64
TOP_K = 8
MOE_H = 512
ROUTED_SCALE = 2.5
BLK = 128

OFF_QKV_A = 0
OFF_QKV_C = 2304
OFF_Z = 3456
OFF_XBC = 4480
OFF_DT = 6016
OFF_GATES = 6032

TM_TOK = 256
NPT_TOK = NP // TM_TOK
BM = 256
NB = T * TOP_K // BM + N_EXP
P_ROWS = NB * BM
TT_DISP = 128


def _cp(sem, vmem_mb=48):
    return pltpu.CompilerParams(dimension_semantics=sem, vmem_limit_bytes=vmem_mb << 20)


def _sigmoid(x):
    return 1.0 / (1.0 + jnp.exp(-x))


def _silu(x):
    return x * _sigmoid(x)


def _softplus(x):
    return jnp.maximum(x, 0.0) + jnp.log1p(jnp.exp(-jnp.abs(x)))


def _rms(x):
    return x * lax.rsqrt(jnp.mean(x * x, axis=-1, keepdims=True) + EPS)


def _dot(a, b):
    return jnp.dot(a.astype(BF16), b.astype(BF16), preferred_element_type=F32)


def _dot_nt(a, b):
    return lax.dot_general(a.astype(BF16), b.astype(BF16), (((1,), (1,)), ((), ())),
                           preferred_element_type=F32)


def _dot_tn(a, b):
    return lax.dot_general(a.astype(BF16), b.astype(BF16), (((0,), (0,)), ((), ())),
                           preferred_element_type=F32)


def _sel_tile(is_prompt, p_ref, s_ref):
    return jnp.where(is_prompt, p_ref[0:1, :], s_ref[...])


def _ada_kernel(c_ref, w_ref, b_ref, o_ref):
    o_ref[...] = _dot(_silu(c_ref[...]), w_ref[...]) + b_ref[...]


def _ada(c_all, ada_w, ada_b):
    rows = c_all.shape[0]
    tn = 1024
    return pl.pallas_call(
        _ada_kernel,
        grid=(DEPTH, 6 * D // tn),
        in_specs=[pl.BlockSpec((rows, D), lambda l, j: (0, 0)),
                  pl.BlockSpec((None, D, tn), lambda l, j: (l, 0, j)),
                  pl.BlockSpec((None, 1, tn), lambda l, j: (l, 0, j))],
        out_specs=pl.BlockSpec((None, rows, tn), lambda l, j: (l, 0, j)),
        out_shape=jax.ShapeDtypeStruct((DEPTH, rows, 6 * D), F32),
        compiler_params=_cp(("arbitrary", "arbitrary")),
        name="ada",
    )(c_all, ada_w, ada_b.reshape(DEPTH, 1, 6 * D))


def _mod_specs(col):
    return [pl.BlockSpec((8, D), lambda i: (0, col)),
            pl.BlockSpec((TM_TOK, D), lambda i: (jnp.maximum(i - NPT_TOK, 0), col))]


def _prenorm_kernel(x_ref, g_ref, shp_ref, shs_ref, scp_ref, scs_ref, o_ref):
    is_p = pl.program_id(0) < NPT_TOK
    sh = _sel_tile(is_p, shp_ref, shs_ref)
    sc = _sel_tile(is_p, scp_ref, scs_ref)
    y = _rms(x_ref[...]) * g_ref[...]
    o_ref[...] = (y * (1.0 + sc) + sh).astype(o_ref.dtype)


def _prenorm(x, g, mod_p, mod_s):
    return pl.pallas_call(
        _prenorm_kernel,
        grid=(T // TM_TOK,),
        in_specs=[pl.BlockSpec((TM_TOK, D), lambda i: (i, 0)),
                  pl.BlockSpec((1, D), lambda i: (0, 0))] + _mod_specs(0) + _mod_specs(1),
        out_specs=pl.BlockSpec((TM_TOK, D), lambda i: (i, 0)),
        out_shape=jax.ShapeDtypeStruct((T, D), BF16),
        compiler_params=_cp(("parallel",)),
        name="prenorm",
    )(x, g.reshape(1, D), mod_p, mod_s, mod_p, mod_s)


def _mm_kernel(x_ref, w_ref, o_ref):
    o_ref[...] = _dot(x_ref[...], w_ref[...]).astype(o_ref.dtype)


def _matmul(x, w, tm, tn, name):
    m, k = x.shape
    n = w.shape[1]
    return pl.pallas_call(
        _mm_kernel,
        grid=(m // tm, n // tn),
        in_specs=[pl.BlockSpec((tm, k), lambda i, j: (i, 0)),
                  pl.BlockSpec((k, tn), lambda i, j: (0, j))],
        out_specs=pl.BlockSpec((tm, tn), lambda i, j: (i, j)),
        out_shape=jax.ShapeDtypeStruct((m, n), F32),
        compiler_params=_cp(("parallel", "arbitrary")),
        name=name,
    )(x, w)


def _banded_kernel(*refs, nh, k_off, v_off, with_sink, with_lse):
    refs = list(refs)
    sink_ref = refs.pop(0) if with_sink else None
    q_ref, kc_ref, kp_ref, vc_ref, vp_ref = refs[:5]
    o_ref = refs[5]
    lse_ref = refs[6] if with_lse else None
    has_prev = pl.program_id(1) > 0
    row = lax.broadcasted_iota(I32, (BLK, 2 * BLK), 0)
    col = lax.broadcasted_iota(I32, (BLK, 2 * BLK), 1)
    dist = row + BLK - col
    valid = (dist >= 0) & (dist <= BLK) & (has_prev | (col >= BLK))
    q = q_ref[...]
    kc, kp, vc, vp = kc_ref[...], kp_ref[...], vc_ref[...], vp_ref[...]
    for h in range(nh):
        qh = q[:, h * HD:(h + 1) * HD]
        ko, vo = k_off[h], v_off[h]
        kk = jnp.concatenate([kp[:, ko:ko + HD], kc[:, ko:ko + HD]], axis=0)
        vv = jnp.concatenate([vp[:, vo:vo + HD], vc[:, vo:vo + HD]], axis=0)
        s = _dot_nt(qh, kk) * (1.0 / math.sqrt(HD))
        s = jnp.where(valid, s, -jnp.inf)
        m = jnp.max(s, axis=-1, keepdims=True)
        if with_sink:
            sk = sink_ref[h]
            m = jnp.maximum(m, sk)
        e = jnp.exp(s - m)
        den = jnp.sum(e, axis=-1, keepdims=True)
        if with_sink:
            den = den + jnp.exp(sk - m)
        o = _dot(e / den, vv)
        o_ref[:, h * HD:(h + 1) * HD] = o.astype(o_ref.dtype)
        if with_lse:
            lse_ref[:, h * HD:(h + 1) * HD] = jnp.broadcast_to(m + jnp.log(den), (BLK, HD))


def _banded_a(qkv, gi, dil):
    view = qkv.reshape(T // dil, dil * QKV_A)
    nb = NP // dil // BLK
    cpr = QKV_A // A_W

    def spec(off, prev):
        if prev:
            return pl.BlockSpec((BLK, A_W), lambda r, j: (jnp.maximum(j - 1, 0), r * cpr + off + gi))
        return pl.BlockSpec((BLK, A_W), lambda r, j: (j, r * cpr + off + gi))

    offs = tuple(h * HD for h in range(A_HEADS))
    out_spec = pl.BlockSpec((BLK, A_W), lambda r, j: (j, r))
    o, lse = pl.pallas_call(
        functools.partial(_banded_kernel, nh=A_HEADS, k_off=offs, v_off=offs,
                          with_sink=False, with_lse=True),
        grid=(dil, nb),
        in_specs=[spec(0, False), spec(3, False), spec(3, True), spec(6, False), spec(6, True)],
        out_specs=[out_spec, out_spec],
        out_shape=[jax.ShapeDtypeStruct((NP // dil, dil * A_W), F32)] * 2,
        compiler_params=_cp(("parallel", "arbitrary")),
        name=f"banded_a{gi}",
    )(view, view, view, view, view)
    return o.reshape(NP, A_W), lse.reshape(NP, A_W)


def _banded_c(qkv, sinks):
    nq = C_HEADS * HD
    kvw = 2 * C_KV * HD
    k_off = tuple((h // 4) * HD for h in range(C_HEADS))
    v_off = tuple(C_KV * HD + (h // 4) * HD for h in range(C_HEADS))
    cur = pl.BlockSpec((BLK, kvw), lambda r, j: (j, nq // kvw))
    prev = pl.BlockSpec((BLK, kvw), lambda r, j: (jnp.maximum(j - 1, 0), nq // kvw))
    return pl.pallas_call(
        functools.partial(_banded_kernel, nh=C_HEADS, k_off=k_off, v_off=v_off,
                          with_sink=True, with_lse=False),
        grid=(1, NP // BLK),
        in_specs=[pl.BlockSpec(memory_space=pltpu.SMEM),
                  pl.BlockSpec((BLK, nq), lambda r, j: (j, 0)), cur, prev, cur, prev],
        out_specs=pl.BlockSpec((BLK, nq), lambda r, j: (j, 0)),
        out_shape=jax.ShapeDtypeStruct((NP, nq), BF16),
        compiler_params=_cp(("parallel", "arbitrary")),
        name="banded_c",
    )(sinks, qkv, qkv, qkv, qkv, qkv)


def _combine_kernel(o1, o2, o3, l1, l2, l3, out_ref):
    a, b, c = l1[...], l2[...], l3[...]
    m = jnp.maximum(jnp.maximum(a, b), c)
    ea, eb, ec = jnp.exp(a - m), jnp.exp(b - m), jnp.exp(c - m)
    tot = ea + eb + ec
    out_ref[...] = ((ea / tot) * o1[...] + (eb / tot) * o2[...] + (ec / tot) * o3[...]).astype(out_ref.dtype)


def _combine_groups(outs, lses):
    tm = 512
    spec = pl.BlockSpec((tm, A_W), lambda i: (i, 0))
    return pl.pallas_call(
        _combine_kernel,
        grid=(NP // tm,),
        in_specs=[spec] * 6,
        out_specs=spec,
        out_shape=jax.ShapeDtypeStruct((NP, A_W), BF16),
        compiler_params=_cp(("parallel",)),
        name="combine_groups",
    )(*outs, *lses)


def _dec_scores(q, nq, nk, kt, knew, dil, sink_col):
    rep = nq // nk
    w = kt.shape[1]
    qrows = jnp.concatenate([q[:, h * HD:(h + 1) * HD] for h in range(nq)], axis=0)
    qt = jnp.concatenate([qrows] * nk, axis=1) if nk > 1 else qrows
    r_i = lax.broadcasted_iota(I32, qt.shape, 0)
    l_i = lax.broadcasted_iota(I32, qt.shape, 1)
    qbd = jnp.where((r_i // (S * rep)) == (l_i // HD), qt, 0.0)
    scale = 1.0 / math.sqrt(HD)
    sc = _dot(qbd, kt) * scale
    tok = lax.broadcasted_iota(I32, sc.shape, 0) % S
    back = lax.broadcasted_iota(I32, sc.shape, 1) - tok
    sc = jnp.where((back >= 0) & ((back & (dil - 1)) == 0), sc, -jnp.inf)
    sn = _dot_nt(qbd, knew) * scale
    back = lax.broadcasted_iota(I32, sn.shape, 0) % S - lax.broadcasted_iota(I32, sn.shape, 1)
    sn = jnp.where((back >= 0) & ((back & (dil - 1)) == 0), sn, -jnp.inf)
    m = jnp.maximum(jnp.max(sc, axis=1, keepdims=True), jnp.max(sn, axis=1, keepdims=True))
    if sink_col is not None:
        m = jnp.maximum(m, sink_col)
    ec, en = jnp.exp(sc - m), jnp.exp(sn - m)
    den = jnp.sum(ec, axis=1, keepdims=True) + jnp.sum(en, axis=1, keepdims=True)
    if sink_col is not None:
        den = den + jnp.exp(sink_col - m)
    return ec, en, m, den


def _pick_heads(o_all, nq, nk):
    rep = nq // nk
    cols = []
    for h in range(nq):
        kh = h // rep
        cols.append(o_all[h * S:(h + 1) * S, kh * HD:(kh + 1) * HD])
    return jnp.concatenate(cols, axis=1)


def _shift_cache(new_ref, old, kv_new):
    rows, w = old.shape
    nblk = w // BLK
    keep = BLK - S
    lane = lax.broadcasted_iota(I32, (rows, BLK), 1)
    left = [pltpu.roll(old[:, j * BLK:(j + 1) * BLK], keep, axis=1) for j in range(nblk)]
    tail = jnp.concatenate([jnp.zeros((keep, rows), F32), kv_new], axis=0).T
    left.append(tail)
    shape = new_ref.shape[:-1] + (BLK,)
    for j in range(nblk):
        new_ref[:, :, :, j * BLK:(j + 1) * BLK] = jnp.where(lane < keep, left[j], left[j + 1]).reshape(shape)


def _decode_attn_kernel(sink_ref, qa_ref, qc_ref, c1_ref, c2_ref, c3_ref, cc_ref, *rest):
    n_alias = len(rest) - 6
    n1_ref, n2_ref, n3_ref, nc_ref, oa_ref, oc_ref = rest[n_alias:]
    qa = qa_ref[...]
    qc = qc_ref[...]
    caches = (c1_ref, c2_ref, c3_ref)
    news = (n1_ref, n2_ref, n3_ref)
    grp = []
    for gi, (w, dil) in enumerate(A_PATTERN):
        q = qa[:, gi * A_W:(gi + 1) * A_W]
        knew = qa[:, 768 + gi * A_W:768 + (gi + 1) * A_W]
        vnew = qa[:, 1536 + gi * A_W:1536 + (gi + 1) * A_W]
        old = caches[gi][...].reshape(2 * A_W, w)
        ec, en, m, den = _dec_scores(q, A_HEADS, A_HEADS, old[0:A_W], knew, dil, None)
        grp.append((ec, en, m + jnp.log(den), den, old[A_W:2 * A_W], vnew))
        _shift_cache(news[gi], old, jnp.concatenate([knew, vnew], axis=1))
    lmax = functools.reduce(jnp.maximum, [g[2] for g in grp])
    gws = [jnp.exp(g[2] - lmax) for g in grp]
    gtot = functools.reduce(jnp.add, gws)
    o_all = None
    for gi, (ec, en, _, den, vt, vnew) in enumerate(grp):
        wcol = (gws[gi] / gtot) / den
        t = _dot_nt(ec * wcol, vt) + _dot(en * wcol, vnew)
        o_all = t if o_all is None else o_all + t
    oa_ref[...] = _pick_heads(o_all, A_HEADS, A_HEADS).astype(oa_ref.dtype)
    nqc = C_HEADS * HD
    kvw = C_KV * HD
    old = cc_ref[...].reshape(2 * kvw, BLK)
    knew = qc[:, nqc:nqc + kvw]
    vnew = qc[:, nqc + kvw:nqc + 2 * kvw]
    ec, en, m, den = _dec_scores(qc[:, 0:nqc], C_HEADS, C_KV, old[0:kvw], knew, 1, sink_ref[...])
    o_c = _dot_nt(ec / den, old[kvw:2 * kvw]) + _dot(en / den, vnew)
    oc_ref[...] = _pick_heads(o_c, C_HEADS, C_KV).astype(oc_ref.dtype)
    _shift_cache(nc_ref, old, qc[:, nqc:nqc + 2 * kvw])


def _decode_attn(layer, sink_col, qkv_a, qkv_c, caches, prev_out):
    row0 = NP // S

    def cache_spec(c):
        return pl.BlockSpec((None, None) + c.shape[2:], lambda b: (layer, b, 0, 0, 0, 0))

    in_specs = [pl.BlockSpec((C_HEADS * S, 1), lambda b: (0, 0)),
                pl.BlockSpec((S, QKV_A), lambda b: (row0 + b, 0)),
                pl.BlockSpec((S, QKV_C), lambda b: (row0 + b, 0))] + [cache_spec(c) for c in caches]
    args = [sink_col, qkv_a, qkv_c] + list(caches)
    aliases = {}
    if prev_out is not None:
        for k, p in enumerate(prev_out):
            in_specs.append(pl.BlockSpec(memory_space=pl.ANY))
            args.append(p)
            aliases[len(args) - 1] = k
    out_specs = [cache_spec(c) for c in caches] + [
        pl.BlockSpec((S, A_W), lambda b: (b, 0)),
        pl.BlockSpec((S, C_HEADS * HD), lambda b: (b, 0))]
    out_shape = [jax.ShapeDtypeStruct(c.shape, F32) for c in caches] + [
        jax.ShapeDtypeStruct((NS, A_W), BF16), jax.ShapeDtypeStruct((NS, C_HEADS * HD), BF16)]
    return pl.pallas_call(
        _decode_attn_kernel,
        grid=(NSEQ,),
        in_specs=in_specs,
        out_specs=out_specs,
        out_shape=out_shape,
        input_output_aliases=aliases,
        compiler_params=_cp(("arbitrary",), 56),
        name=f"decode_attn{layer}",
    )(*args)


def _ssd_kernel(xbcdt_ref, z_ref, conv0_ref, h0_ref, convw_ref, convb_ref, dtb_ref, arow_ref,
                dskip_ref, nrm_ref, y_ref, hout_ref, xp_ref, h_ref, yacc_ref, *, q_len):
    c = pl.program_id(1)
    Q = q_len

    @pl.when(c == 0)
    def _():
        xp_ref[0:8, :] = conv0_ref[...]
        h_ref[...] = h0_ref[...]

    xbc = xbcdt_ref[:, 0:CONV_DIM]
    xp_ref[8:8 + Q, :] = xbc
    w = convw_ref[...]
    acc = convb_ref[...] + w[3:4, :] * xbc
    acc = acc + w[2:3, :] * xp_ref[7:7 + Q, :]
    acc = acc + w[1:2, :] * xp_ref[6:6 + Q, :]
    acc = acc + w[0:1, :] * xp_ref[5:5 + Q, :]
    u = _silu(acc)
    xp_ref[0:8, :] = xbc[Q - 8:Q, :]

    dtp = _softplus(xbcdt_ref[:, CONV_DIM:XBCDT_W] + dtb_ref[...])
    da = dtp * arow_ref[...]
    ri = lax.broadcasted_iota(I32, (Q, Q), 0)
    ci = lax.broadcasted_iota(I32, (Q, Q), 1)
    tri = ri >= ci
    acs = jnp.dot(tri.astype(F32), da, precision=lax.Precision.HIGHEST, preferred_element_type=F32)
    acs_t = acs.T
    dt_t = dtp.T
    hpg = SSM_HEADS // 2
    for g in range(2):
        bg = u[:, SSM_INNER + g * SSM_N:SSM_INNER + (g + 1) * SSM_N]
        cg = u[:, SSM_INNER + (2 + g) * SSM_N:SSM_INNER + (3 + g) * SSM_N]
        cb = _dot_nt(cg, bg)
        for hh in range(hpg):
            hd = g * hpg + hh
            a_col = acs[:, hd:hd + 1]
            seg = a_col - acs_t[hd:hd + 1, :]
            decay = jnp.exp(jnp.where(tri, seg, -jnp.inf))
            mh = cb * decay * dt_t[hd:hd + 1, :]
            xh = u[:, hd * HD:(hd + 1) * HD]
            hst = h_ref[hd]
            y = _dot(mh, xh) + _dot_nt(cg, hst) * jnp.exp(a_col)
            a_last = acs[Q - 1:Q, hd:hd + 1]
            tail = jnp.exp(a_last - a_col) * dtp[:, hd:hd + 1]
            h_ref[hd] = hst * jnp.exp(a_last) + _dot_tn(xh * tail, bg)
            yacc_ref[:, hd * HD:(hd + 1) * HD] = y + dskip_ref[:, hd * HD:(hd + 1) * HD] * xh
    yz = yacc_ref[...] * _silu(z_ref[...])
    half = SSM_INNER // 2
    for g in range(2):
        yg = yz[:, g * half:(g + 1) * half]
        y_ref[:, g * half:(g + 1) * half] = (_rms(yg) * nrm_ref[:, g * half:(g + 1) * half]).astype(y_ref.dtype)

    @pl.when(c == pl.num_programs(1) - 1)
    def _():
        hout_ref[...] = h_ref[...]


def _ssd(layer, xbcdt, z, conv0, h0, h0_layer, convw, convb, dtb, arow, dskip, nrm, nseq, q_len, row0):
    nchunks = (NP if nseq == 1 else S) // q_len
    rows = nseq * nchunks * q_len
    vec = lambda n: pl.BlockSpec((1, n), lambda b, c: (0, 0))
    return pl.pallas_call(
        functools.partial(_ssd_kernel, q_len=q_len),
        grid=(nseq, nchunks),
        in_specs=[pl.BlockSpec((q_len, XBCDT_W), lambda b, c: (row0 + b * nchunks + c, 0)),
                  pl.BlockSpec((q_len, SSM_INNER), lambda b, c: (row0 + b * nchunks + c, 0)),
                  pl.BlockSpec((None, 8, CONV_DIM), lambda b, c: (b, 0, 0)),
                  pl.BlockSpec((None, None, SSM_HEADS, HD, SSM_N), lambda b, c: (h0_layer, b, 0, 0, 0)),
                  pl.BlockSpec((4, CONV_DIM), lambda b, c: (0, 0)),
                  vec(CONV_DIM), vec(128), vec(128), vec(SSM_INNER), vec(SSM_INNER)],
        out_specs=[pl.BlockSpec((q_len, SSM_INNER), lambda b, c: (b * nchunks + c, 0)),
                   pl.BlockSpec((None, SSM_HEADS, HD, SSM_N), lambda b, c: (b, 0, 0, 0))],
        out_shape=[jax.ShapeDtypeStruct((rows, SSM_INNER), BF16),
                   jax.ShapeDtypeStruct((nseq, SSM_HEADS, HD, SSM_N), F32)],
        scratch_shapes=[pltpu.VMEM((q_len + 8, CONV_DIM), F32),
                        pltpu.VMEM((SSM_HEADS, HD, SSM_N), F32),
                        pltpu.VMEM((q_len, SSM_INNER), F32)],
        compiler_params=_cp(("arbitrary", "arbitrary")),
        name=f"ssd_q{q_len}",
    )(xbcdt, z, conv0, h0, convw, convb, dtb, arow, dskip, nrm)


def _merge_kernel(h_ref, oa_ref, ob_ref, oc_ref, g0_ref, g1_ref, g2_ref, wa_ref, wb_ref, wc_ref, o_ref):
    h = h_ref[...]
    acc = _sigmoid(_dot(h, g0_ref[...])) * _dot(oa_ref[...], wa_ref[...])
    acc = acc + _sigmoid(_dot(h, g1_ref[...])) * _dot(ob_ref[...], wb_ref[...])
    acc = acc + _sigmoid(_dot(h, g2_ref[...])) * _dot(oc_ref[...], wc_ref[...])
    o_ref[...] = acc.astype(o_ref.dtype)


def _merge(layer, h1, oa, ob, oc, wgate, w_br_a, w_br_b, w_br_c):
    tm, tn = 1024, 512
    nj = D // tn
    row = lambda k: pl.BlockSpec((tm, k), lambda i, j: (i, 0))
    gate = lambda b: pl.BlockSpec((D, tn), lambda i, j: (0, b * nj + j))
    br = lambda k: pl.BlockSpec((None, k, tn), lambda i, j: (layer, 0, j))
    return pl.pallas_call(
        _merge_kernel,
        grid=(T // tm, nj),
        in_specs=[row(D), row(A_W), row(SSM_INNER), row(C_HEADS * HD), gate(0), gate(1), gate(2),
                  br(A_W), br(SSM_INNER), br(C_HEADS * HD)],
        out_specs=pl.BlockSpec((tm, tn), lambda i, j: (i, j)),
        out_shape=jax.ShapeDtypeStruct((T, D), BF16),
        compiler_params=_cp(("parallel", "arbitrary"), 56),
        name="merge",
    )(h1, oa, ob, oc, wgate, wgate, wgate, w_br_a, w_br_b, w_br_c)


def _outproj_kernel(m_ref, w_ref, x_ref, gpost_ref, gpre_ref, gap_ref, gas_ref, shp_ref, shs_ref,
                    scp_ref, scs_ref, x_out_ref, hf_ref):
    is_p = pl.program_id(0) < NPT_TOK
    ga = _sel_tile(is_p, gap_ref, gas_ref)
    sh = _sel_tile(is_p, shp_ref, shs_ref)
    sc = _sel_tile(is_p, scp_ref, scs_ref)
    mix = _dot(m_ref[...], w_ref[...])
    xn = x_ref[...] + ga * (_rms(mix) * gpost_ref[...])
    x_out_ref[...] = xn
    hf_ref[...] = (_rms(xn) * gpre_ref[...]) * (1.0 + sc) + sh


def _outproj(merged, w_out_bf, x, g_post, g_pre, mod_p, mod_s):
    tok = pl.BlockSpec((TM_TOK, D), lambda i: (i, 0))
    vec = pl.BlockSpec((1, D), lambda i: (0, 0))
    return pl.pallas_call(
        _outproj_kernel,
        grid=(T // TM_TOK,),
        in_specs=[tok, pl.BlockSpec((D, D), lambda i: (0, 0)), tok, vec, vec]
        + _mod_specs(2) + _mod_specs(3) + _mod_specs(4),
        out_specs=[tok, tok],
        out_shape=[jax.ShapeDtypeStruct((T, D), F32)] * 2,
        compiler_params=_cp(("parallel",), 56),
        name="outproj",
    )(merged, w_out_bf, x, g_post.reshape(1, D), g_pre.reshape(1, D), mod_p, mod_s, mod_p, mod_s, mod_p, mod_s)


def _router_kernel(hf_ref, rwt_ref, bias_ref, topi_ref, topw_ref):
    tm = hf_ref.shape[0]
    logits = lax.dot_general(rwt_ref[...], hf_ref[...], (((1,), (1,)), ((), ())),
                             precision=lax.Precision.HIGHEST, preferred_element_type=F32)
    scores = _sigmoid(logits)
    sel = scores + bias_ref[...]
    ng = 8
    sel3 = sel.reshape(ng, N_EXP // ng, tm)
    idx3 = lax.broadcasted_iota(I32, sel3.shape, 1)
    m1 = jnp.max(sel3, axis=1, keepdims=True)
    first = jnp.min(jnp.where(sel3 == m1, idx3, N_EXP), axis=1, keepdims=True)
    m2 = jnp.max(jnp.where(idx3 == first, -jnp.inf, sel3), axis=1, keepdims=True)
    gs = m1 + m2
    gidx = lax.broadcasted_iota(I32, (ng, 1, tm), 0)
    gmask = jnp.zeros((ng, 1, tm), jnp.bool_)
    for _ in range(4):
        m = jnp.max(gs, axis=0, keepdims=True)
        f = jnp.min(jnp.where(gs == m, gidx, ng), axis=0, keepdims=True)
        hit = gidx == f
        gmask = jnp.logical_or(gmask, hit)
        gs = jnp.where(hit, -jnp.inf, gs)
    masked = jnp.where(gmask, sel3, -jnp.inf).reshape(N_EXP, tm)
    eidx = lax.broadcasted_iota(I32, (N_EXP, tm), 0)
    tis, tws = [], []
    for _ in range(TOP_K):
        m = jnp.max(masked, axis=0, keepdims=True)
        f = jnp.min(jnp.where(masked == m, eidx, N_EXP), axis=0, keepdims=True)
        hit = eidx == f
        tis.append(f)
        tws.append(jnp.sum(jnp.where(hit, scores, 0.0), axis=0, keepdims=True))
        masked = jnp.where(hit, -jnp.inf, masked)
    wsum = functools.reduce(jnp.add, tws)
    topi_ref[...] = jnp.concatenate(tis, axis=0)
    topw_ref[...] = jnp.concatenate([w / wsum * ROUTED_SCALE for w in tws], axis=0)


def _router(hf, rwt, bias_col):
    tm = 512
    return pl.pallas_call(
        _router_kernel,
        grid=(T // tm,),
        in_specs=[pl.BlockSpec((tm, D), lambda i: (i, 0)),
                  pl.BlockSpec((N_EXP, D), lambda i: (0, 0)),
                  pl.BlockSpec((N_EXP, 1), lambda i: (0, 0))],
        out_specs=[pl.BlockSpec((TOP_K, tm), lambda i: (0, i))] * 2,
        out_shape=[jax.ShapeDtypeStruct((TOP_K, T), I32), jax.ShapeDtypeStruct((TOP_K, T), F32)],
        compiler_params=_cp(("parallel",)),
        name="router",
    )(hf, rwt, bias_col)


def _route_tables(topi):
    onehot = topi[None] == jnp.arange(N_EXP, dtype=I32)[:, None, None]
    tok = jnp.any(onehot, axis=1).astype(I32)
    incl = jnp.cumsum(tok, axis=1)
    counts = incl[:, -1]
    padded = (counts + BM - 1) // BM * BM
    pad_end = jnp.cumsum(padded)
    base = (pad_end - padded)[:, None] + incl - tok
    dest = jnp.sum(jnp.where(onehot, base[:, None, :], 0), axis=0).astype(I32)
    starts = jnp.arange(NB, dtype=I32) * BM
    block_e = jnp.minimum(jnp.sum(pad_end[None, :] <= starts[:, None], axis=1), N_EXP - 1).astype(I32)
    n_used = (pad_end[-1] // BM).astype(I32).reshape(1)
    return dest, block_e, n_used


def _dispatch_kernel(dest_ref, hf_ref, xs_in_ref, xs_ref, sem):
    del xs_in_ref

    def body(t, carry):
        for k in range(TOP_K):
            pltpu.make_async_copy(hf_ref.at[pl.ds(t, 1)], xs_ref.at[pl.ds(dest_ref[k, t], 1)], sem).start()
        return carry

    lax.fori_loop(0, TT_DISP, body, 0)
    for k in range(TOP_K):
        pltpu.make_async_copy(hf_ref, xs_ref.at[pl.ds(0, TT_DISP)], sem).wait()


def _dispatch(dest, hf, xs_buf):
    return pl.pallas_call(
        _dispatch_kernel,
        grid=(T // TT_DISP,),
        in_specs=[pl.BlockSpec((TOP_K, TT_DISP), lambda i: (0, i), memory_space=pltpu.SMEM),
                  pl.BlockSpec((TT_DISP, D), lambda i: (i, 0)),
                  pl.BlockSpec(memory_space=pl.ANY)],
        out_specs=pl.BlockSpec(memory_space=pl.ANY),
        out_shape=jax.ShapeDtypeStruct((P_ROWS, D), F32),
        scratch_shapes=[pltpu.SemaphoreType.DMA(())],
        input_output_aliases={2: 0},
        compiler_params=_cp(("arbitrary",)),
        name="moe_dispatch",
    )(dest, hf, xs_buf)


def _swiglu(x, wg, wu, wd):
    return _dot(_silu(_dot(x, wg)) * _dot(x, wu), wd)


def _expert_kernel(be_ref, nu_ref, x_ref, wg_ref, wu_ref, wd_ref, o_ref):
    del be_ref
    used = pl.program_id(0) < nu_ref[0]

    @pl.when(used)
    def _():
        o_ref[...] = _swiglu(x_ref[...], wg_ref[...], wu_ref[...], wd_ref[...])

    @pl.when(jnp.logical_not(used))
    def _():
        o_ref[...] = jnp.zeros_like(o_ref)


def _experts(layer, xs, block_e, n_used, exp_gate, exp_up, exp_down):
    wspec = lambda a, b: pl.BlockSpec((None, None, a, b), lambda i, be, nu: (layer, be[i], 0, 0))
    return pl.pallas_call(
        _expert_kernel,
        grid_spec=pltpu.PrefetchScalarGridSpec(
            num_scalar_prefetch=2,
            grid=(NB,),
            in_specs=[pl.BlockSpec((BM, D), lambda i, be, nu: (i, 0)),
                      wspec(D, MOE_H), wspec(D, MOE_H), wspec(MOE_H, D)],
            out_specs=pl.BlockSpec((BM, D), lambda i, be, nu: (i, 0))),
        out_shape=jax.ShapeDtypeStruct((P_ROWS, D), F32),
        compiler_params=_cp(("arbitrary",), 56),
        name="moe_experts",
    )(block_e, n_used, xs, exp_gate, exp_up, exp_down)


def _shared_kernel(x_ref, wg_ref, wu_ref, wd_ref, o_ref):
    o_ref[...] = _swiglu(x_ref[...], wg_ref[...], wu_ref[...], wd_ref[...])


def _shared_expert(layer, hf, wg, wu, wd):
    tm = 512
    wspec = lambda a, b: pl.BlockSpec((None, a, b), lambda i: (layer, 0, 0))
    return pl.pallas_call(
        _shared_kernel,
        grid=(T // tm,),
        in_specs=[pl.BlockSpec((tm, D), lambda i: (i, 0)), wspec(D, MOE_H), wspec(D, MOE_H), wspec(MOE_H, D)],
        out_specs=pl.BlockSpec((tm, D), lambda i: (i, 0)),
        out_shape=jax.ShapeDtypeStruct((T, D), F32),
        compiler_params=_cp(("parallel",), 56),
        name="moe_shared",
    )(hf, wg, wu, wd)


def _moe_combine_kernel(dest_ref, ys_ref, w_ref, fsh_ref, x_ref, gpost_ref, gap_ref, gas_ref,
                        o_ref, gbuf, sem):
    tt = TT_DISP
    is_p = pl.program_id(0) < NP // tt

    def body(t, carry):
        for k in range(TOP_K):
            pltpu.make_async_copy(ys_ref.at[pl.ds(dest_ref[k, t], 1)], gbuf.at[k, pl.ds(t, 1)], sem).start()
        return carry

    lax.fori_loop(0, tt, body, 0)
    for k in range(TOP_K):
        pltpu.make_async_copy(ys_ref.at[pl.ds(0, tt)], gbuf.at[k], sem).wait()
    w = w_ref[...]
    f = fsh_ref[...]
    for k in range(TOP_K):
        f = f + gbuf[k] * w[:, k:k + 1]
    ga = jnp.where(is_p, gap_ref[0:1, :], gas_ref[...])
    o_ref[...] = x_ref[...] + ga * (_rms(f) * gpost_ref[...])


def _moe_combine(dest, ys, topw_t, f_shared, x, g_post, mod_p, mod_s):
    tt = TT_DISP
    tok = pl.BlockSpec((tt, D), lambda i: (i, 0))
    return pl.pallas_call(
        _moe_combine_kernel,
        grid=(T // tt,),
        in_specs=[pl.BlockSpec((TOP_K, tt), lambda i: (0, i), memory_space=pltpu.SMEM),
                  pl.BlockSpec(memory_space=pl.ANY),
                  pl.BlockSpec((tt, TOP_K), lambda i: (i, 0)),
                  tok, tok, pl.BlockSpec((1, D), lambda i: (0, 0)),
                  pl.BlockSpec((8, D), lambda i: (0, 5)),
                  pl.BlockSpec((tt, D), lambda i: (jnp.maximum(i - NP // tt, 0), 5))],
        out_specs=tok,
        out_shape=jax.ShapeDtypeStruct((T, D), F32),
        scratch_shapes=[pltpu.VMEM((TOP_K, tt, D), F32), pltpu.SemaphoreType.DMA(())],
        compiler_params=_cp(("arbitrary",)),
        name="moe_combine",
    )(dest, ys, topw_t, f_shared, x, g_post.reshape(1, D), mod_p, mod_s)


def _pad_cols(w, n):
    return jnp.pad(w, ((0, 0), (0, n - w.shape[1])))


def _layer(l, x, mod_p, mod_s, p, dec_caches, dec_prev, xs_buf):
    w_in = p['w_in'][l]
    h1 = _prenorm(x, p['norm_pre_mix'][l], mod_p, mod_s)
    qkv_a = _matmul(h1, w_in[:, OFF_QKV_A:OFF_QKV_C].astype(BF16), 1024, 768, "proj_qkv_a")
    qkv_c = _matmul(h1, w_in[:, OFF_QKV_C:OFF_Z].astype(BF16), 1024, QKV_C, "proj_qkv_c")
    z = _matmul(h1, w_in[:, OFF_Z:OFF_XBC].astype(BF16), 1024, SSM_INNER, "proj_z")
    xbcdt = _matmul(h1, _pad_cols(w_in[:, OFF_XBC:OFF_GATES], XBCDT_W).astype(BF16), 1024, XBCDT_W, "proj_xbcdt")

    outs, lses = zip(*[_banded_a(qkv_a, gi, dil) for gi, (_, dil) in enumerate(A_PATTERN)])
    oa_p = _combine_groups(outs, lses)
    oc_p = _banded_c(qkv_c, p['sinks'][l])
    sink_col = jnp.repeat(p['sinks'][l], S).reshape(C_HEADS * S, 1)
    n1, n2, n3, ncc, oa_s, oc_s = _decode_attn(l, sink_col, qkv_a, qkv_c, dec_caches, dec_prev)

    dtb = jnp.pad(p['dt_bias'][l], (0, 128 - SSM_HEADS)).reshape(1, 128)
    arow = jnp.pad(-jnp.exp(p['a_log'][l]), (0, 128 - SSM_HEADS)).reshape(1, 128)
    dskip = jnp.repeat(p['d_skip'][l], HD).reshape(1, SSM_INNER)
    nrm = p['ssm_norm'][l].reshape(1, SSM_INNER)
    convw = p['conv_w'][l]
    convb = p['conv_b'][l].reshape(1, CONV_DIM)
    ob_p, ssm_p = _ssd(l, xbcdt, z, jnp.zeros((1, 8, CONV_DIM), F32),
                       jnp.zeros((1, 1, SSM_HEADS, HD, SSM_N), F32), 0,
                       convw, convb, dtb, arow, dskip, nrm, 1, BLK, 0)
    conv0_s = jnp.pad(p['state_conv'][l], ((0, 0), (5, 0), (0, 0)))
    ob_s, ssm_s = _ssd(l, xbcdt, z, conv0_s, p['state_ssm'], l,
                       convw, convb, dtb, arow, dskip, nrm, NSEQ, S, NP // S)

    oa = jnp.concatenate([oa_p, oa_s], axis=0)
    ob = jnp.concatenate([ob_p, ob_s], axis=0)
    oc = jnp.concatenate([oc_p, oc_s], axis=0)
    merged = _merge(l, h1, oa, ob, oc, w_in[:, OFF_GATES:].astype(BF16), p['w_br_a'], p['w_br_b'], p['w_br_c'])
    x, hf = _outproj(merged, p['w_out'][l].astype(BF16), x, p['norm_post_mix'][l], p['norm_pre_ffn'][l],
                     mod_p, mod_s)

    topi, topw = _router(hf, p['router_w'][l].T, p['router_bias'][l].reshape(N_EXP, 1))
    dest, block_e, n_used = _route_tables(topi)
    xs = _dispatch(dest, hf, xs_buf)
    ys = _experts(l, xs, block_e, n_used, p['exp_gate'], p['exp_up'], p['exp_down'])
    f_shared = _shared_expert(l, hf, p['shared_gate'], p['shared_up'], p['shared_down'])
    x = _moe_combine(dest, ys, topw.T, f_shared, x, p['norm_post_ffn'][l], mod_p, mod_s)

    def last_kv(src, k_off, v_off, rows, heads):
        k = src[NP - rows:NP, k_off:k_off + heads * HD].reshape(rows, heads, HD)
        v = src[NP - rows:NP, v_off:v_off + heads * HD].reshape(rows, heads, HD)
        return jnp.stack([k, v], axis=1)[None]

    a_states = [last_kv(qkv_a, 768 + gi * A_W, 1536 + gi * A_W, w, A_HEADS)
                for gi, (w, _) in enumerate(A_PATTERN)]
    c_state = last_kv(qkv_c, C_HEADS * HD, C_HEADS * HD + C_KV * HD, BLK, C_KV)
    conv_p = xbcdt[NP - 3:NP, :CONV_DIM][None]
    conv_s = xbcdt[NP:, :CONV_DIM].reshape(NSEQ, S, CONV_DIM)[:, S - 3:]
    prompt_state = (*a_states, c_state, ssm_p.reshape(1, 2, SSM_HEADS // 2, HD, SSM_N), conv_p)
    return x, prompt_state, (n1, n2, n3, ncc), ssm_s, conv_s, xs


def kernel(x_prompt, x_sample, cache_a1_kv, cache_a2_kv, cache_a3_kv, cache_c_kv, state_ssm, state_conv,
           c_prompt, c_sample, ada_w, ada_b, norm_pre_mix, norm_post_mix, norm_pre_ffn, norm_post_ffn,
           w_in, conv_w, conv_b, dt_bias, a_log, d_skip, ssm_norm, sinks, w_br_a, w_br_b, w_br_c, w_out,
           router_w, router_bias, exp_gate, exp_up, exp_down, shared_gate, shared_up, shared_down):
    p = dict(norm_pre_mix=norm_pre_mix, norm_post_mix=norm_post_mix, norm_pre_ffn=norm_pre_ffn,
             norm_post_ffn=norm_post_ffn, w_in=w_in, conv_w=conv_w, conv_b=conv_b, dt_bias=dt_bias,
             a_log=a_log, d_skip=d_skip, ssm_norm=ssm_norm, sinks=sinks, w_br_a=w_br_a, w_br_b=w_br_b,
             w_br_c=w_br_c, w_out=w_out, router_w=router_w, router_bias=router_bias, exp_gate=exp_gate,
             exp_up=exp_up, exp_down=exp_down, shared_gate=shared_gate, shared_up=shared_up,
             shared_down=shared_down, state_conv=state_conv,
             state_ssm=state_ssm.reshape(DEPTH, NSEQ, SSM_HEADS, HD, SSM_N))
    x = jnp.concatenate([x_prompt.reshape(NP, D), x_sample.reshape(NS, D)], axis=0)
    c_all = jnp.concatenate([c_prompt, c_sample, jnp.zeros((7, D), F32)], axis=0)
    mod = _ada(c_all, ada_w, ada_b)
    dec_caches = tuple(jnp.transpose(c, (0, 1, 3, 4, 5, 2))
                       for c in (cache_a1_kv, cache_a2_kv, cache_a3_kv, cache_c_kv))
    xs_buf = jnp.zeros((P_ROWS, D), F32)
    dec_prev = None
    prompt_states, ssm_s, conv_s = [], [], []
    for l in range(DEPTH):
        mod_s = jnp.repeat(mod[l, 1:1 + NSEQ], S, axis=0)
        x, pst, dec_prev, ssm_l, conv_l, xs_buf = _layer(l, x, mod[l], mod_s, p, dec_caches, dec_prev, xs_buf)
        prompt_states.append(pst)
        ssm_s.append(ssm_l)
        conv_s.append(conv_l)
    a1_p, a2_p, a3_p, c_p, ssm_p, conv_p = (jnp.stack(t) for t in zip(*prompt_states))
    n1, n2, n3, ncc = (jnp.transpose(c, (0, 1, 5, 2, 3, 4)) for c in dec_prev)
    y_p = x[:NP].reshape(1, NP, D)
    y_s = x[NP:].reshape(NSEQ, S, D)
    return (y_p, y_s, a1_p, n1, a2_p, n2, a3_p, n3, c_p, ncc,
            ssm_p, jnp.stack(ssm_s).reshape(state_ssm.shape), conv_p, jnp.stack(conv_s))
```

```python
import functools
import math

import jax
import jax.numpy as jnp
from jax import lax
from jax.experimental import pallas as pl
from jax.experimental.pallas import tpu as pltpu

F32 = jnp.float32
BF16 = jnp.bfloat16
I32 = jnp.int32

D = 2048
NP = 8192
NSEQ = 128
S = 8
NS = NSEQ * S
T = NP + NS
DEPTH = 2
HD = 64
EPS = 1e-6
A_PATTERN = ((128, 1), (512, 4), (2048, 16))
A_HEADS = 4
A_W = 256
QKV_A = 2304
C_HEADS = 12
C_KV = 3
QKV_C = 1152
SSM_INNER = 1024
SSM_HEADS = 16
SSM_N = 128
CONV_DIM = 1536
XBCDT_W = 1664
N_EXP = 64
TOP_K = 8
MOE_H = 512
ROUTED_SCALE = 2.5
BLK = 128

OFF_QKV_A = 0
OFF_QKV_C = 2304
OFF_Z = 3456
OFF_XBC = 4480
OFF_DT = 6016
OFF_GATES = 6032

TM_TOK = 256
NPT_TOK = NP // TM_TOK
BM = 256
NB = T * TOP_K // BM + N_EXP
P_ROWS = NB * BM
TT_DISP = 128


def _cp(sem, vmem_mb=48):
    return pltpu.CompilerParams(dimension_semantics=sem, vmem_limit_bytes=vmem_mb << 20)


def _sigmoid(x):
    return 1.0 / (1.0 + jnp.exp(-x))


def _silu(x):
    return x * _sigmoid(x)


def _softplus(x):
    return jnp.maximum(x, 0.0) + jnp.log1p(jnp.exp(-jnp.abs(x)))


def _rms(x):
    return x * lax.rsqrt(jnp.mean(x * x, axis=-1, keepdims=True) + EPS)


def _dot(a, b):
    return jnp.dot(a.astype(BF16), b.astype(BF16), preferred_element_type=F32)


def _dot_nt(a, b):
    return lax.dot_general(a.astype(BF16), b.astype(BF16), (((1,), (1,)), ((), ())),
                           preferred_element_type=F32)


def _dot_tn(a, b):
    return lax.dot_general(a.astype(BF16), b.astype(BF16), (((0,), (0,)), ((), ())),
                           preferred_element_type=F32)


def _pack_bf16_pairs(x):
    half = x.shape[1] // 2
    bits = lax.bitcast_convert_type(x.astype(BF16).astype(F32), jnp.uint32)
    return (bits[:, :half] >> 16) | (bits[:, half:] & jnp.uint32(0xFFFF0000))


def _unpack_bf16_pairs(w):
    lo = lax.bitcast_convert_type(w << 16, F32)
    hi = lax.bitcast_convert_type(w & jnp.uint32(0xFFFF0000), F32)
    return jnp.concatenate([lo.astype(BF16), hi.astype(BF16)], axis=1)


def _sel_tile(is_prompt, p_ref, s_ref):
    return jnp.where(is_prompt, p_ref[0:1, :], s_ref[...])


def _ada_kernel(c_ref, w_ref, b_ref, o_ref):
    o_ref[...] = _dot(_silu(c_ref[...]), w_ref[...]) + b_ref[...]


def _ada(c_all, ada_w, ada_b):
    rows = c_all.shape[0]
    tn = 1024
    return pl.pallas_call(
        _ada_kernel,
        grid=(DEPTH, 6 * D // tn),
        in_specs=[pl.BlockSpec((rows, D), lambda l, j: (0, 0)),
                  pl.BlockSpec((None, D, tn), lambda l, j: (l, 0, j)),
                  pl.BlockSpec((None, 1, tn), lambda l, j: (l, 0, j))],
        out_specs=pl.BlockSpec((None, rows, tn), lambda l, j: (l, 0, j)),
        out_shape=jax.ShapeDtypeStruct((DEPTH, rows, 6 * D), F32),
        compiler_params=_cp(("arbitrary", "arbitrary")),
        name="ada",
    )(c_all, ada_w, ada_b.reshape(DEPTH, 1, 6 * D))


def _mod_specs(col):
    return [pl.BlockSpec((8, D), lambda i: (0, col)),
            pl.BlockSpec((TM_TOK, D), lambda i: (jnp.maximum(i - NPT_TOK, 0), col))]


def _prenorm_kernel(x_ref, g_ref, shp_ref, shs_ref, scp_ref, scs_ref, o_ref):
    is_p = pl.program_id(0) < NPT_TOK
    sh = _sel_tile(is_p, shp_ref, shs_ref)
    sc = _sel_tile(is_p, scp_ref, scs_ref)
    y = _rms(x_ref[...]) * g_ref[...]
    o_ref[...] = (y * (1.0 + sc) + sh).astype(o_ref.dtype)


def _prenorm(x, g, mod_p, mod_s):
    return pl.pallas_call(
        _prenorm_kernel,
        grid=(T // TM_TOK,),
        in_specs=[pl.BlockSpec((TM_TOK, D), lambda i: (i, 0)),
                  pl.BlockSpec((1, D), lambda i: (0, 0))] + _mod_specs(0) + _mod_specs(1),
        out_specs=pl.BlockSpec((TM_TOK, D), lambda i: (i, 0)),
        out_shape=jax.ShapeDtypeStruct((T, D), BF16),
        compiler_params=_cp(("parallel",)),
        name="prenorm",
    )(x, g.reshape(1, D), mod_p, mod_s, mod_p, mod_s)


def _mm_kernel(x_ref, w_ref, o_ref):
    o_ref[...] = _dot(x_ref[...], w_ref[...]).astype(o_ref.dtype)


def _matmul(x, w, tm, tn, name):
    m, k = x.shape
    n = w.shape[1]
    return pl.pallas_call(
        _mm_kernel,
        grid=(m // tm, n // tn),
        in_specs=[pl.BlockSpec((tm, k), lambda i, j: (i, 0)),
                  pl.BlockSpec((k, tn), lambda i, j: (0, j))],
        out_specs=pl.BlockSpec((tm, tn), lambda i, j: (i, j)),
        out_shape=jax.ShapeDtypeStruct((m, n), F32),
        compiler_params=_cp(("parallel", "arbitrary")),
        name=name,
    )(x, w)


def _banded_kernel(*refs, nh, k_off, v_off, with_sink, with_lse):
    refs = list(refs)
    sink_ref = refs.pop(0) if with_sink else None
    q_ref, kc_ref, kp_ref, vc_ref, vp_ref = refs[:5]
    o_ref = refs[5]
    lse_ref = refs[6] if with_lse else None
    has_prev = pl.program_id(1) > 0
    row = lax.broadcasted_iota(I32, (BLK, 2 * BLK), 0)
    col = lax.broadcasted_iota(I32, (BLK, 2 * BLK), 1)
    dist = row + BLK - col
    valid = (dist >= 0) & (dist <= BLK) & (has_prev | (col >= BLK))
    q = q_ref[...]
    kc, kp, vc, vp = kc_ref[...], kp_ref[...], vc_ref[...], vp_ref[...]
    for h in range(nh):
        qh = q[:, h * HD:(h + 1) * HD]
        ko, vo = k_off[h], v_off[h]
        kk = jnp.concatenate([kp[:, ko:ko + HD], kc[:, ko:ko + HD]], axis=0)
        vv = jnp.concatenate([vp[:, vo:vo + HD], vc[:, vo:vo + HD]], axis=0)
        s = _dot_nt(qh, kk) * (1.0 / math.sqrt(HD))
        s = jnp.where(valid, s, -jnp.inf)
        m = jnp.max(s, axis=-1, keepdims=True)
        if with_sink:
            sk = sink_ref[h]
            m = jnp.maximum(m, sk)
        e = jnp.exp(s - m)
        den = jnp.sum(e, axis=-1, keepdims=True)
        if with_sink:
            den = den + jnp.exp(sk - m)
        o = _dot(e / den, vv)
        o_ref[:, h * HD:(h + 1) * HD] = o.astype(o_ref.dtype)
        if with_lse:
            lse_ref[:, h * HD:(h + 1) * HD] = jnp.broadcast_to(m + jnp.log(den), (BLK, HD))


def _banded_a(qkv, gi, dil):
    cols = jnp.concatenate([qkv[:NP, o + gi * A_W:o + (gi + 1) * A_W] for o in (0, 768, 1536)], axis=1)
    view = cols.astype(BF16).reshape(NP // dil, dil * 3 * A_W)
    nb = NP // dil // BLK
    cpr = 3

    def spec(off, prev):
        off = off // 3
        if prev:
            return pl.BlockSpec((BLK, A_W), lambda r, j: (jnp.maximum(j - 1, 0), r * cpr + off))
        return pl.BlockSpec((BLK, A_W), lambda r, j: (j, r * cpr + off))

    offs = tuple(h * HD for h in range(A_HEADS))
    out_spec = pl.BlockSpec((BLK, A_W), lambda r, j: (j, r))
    o, lse = pl.pallas_call(
        functools.partial(_banded_kernel, nh=A_HEADS, k_off=offs, v_off=offs,
                          with_sink=False, with_lse=True),
        grid=(dil, nb),
        in_specs=[spec(0, False), spec(3, False), spec(3, True), spec(6, False), spec(6, True)],
        out_specs=[out_spec, out_spec],
        out_shape=[jax.ShapeDtypeStruct((NP // dil, dil * A_W), F32)] * 2,
        compiler_params=_cp(("parallel", "arbitrary")),
        name=f"banded_a{gi}",
    )(view, view, view, view, view)
    return o.reshape(NP, A_W), lse.reshape(NP, A_W)


def _banded_c(qkv, sinks):
    nq = C_HEADS * HD
    kvw = 2 * C_KV * HD
    k_off = tuple((h // 4) * HD for h in range(C_HEADS))
    v_off = tuple(C_KV * HD + (h // 4) * HD for h in range(C_HEADS))
    cur = pl.BlockSpec((BLK, kvw), lambda r, j: (j, nq // kvw))
    prev = pl.BlockSpec((BLK, kvw), lambda r, j: (jnp.maximum(j - 1, 0), nq // kvw))
    return pl.pallas_call(
        functools.partial(_banded_kernel, nh=C_HEADS, k_off=k_off, v_off=v_off,
                          with_sink=True, with_lse=False),
        grid=(1, NP // BLK),
        in_specs=[pl.BlockSpec(memory_space=pltpu.SMEM),
                  pl.BlockSpec((BLK, nq), lambda r, j: (j, 0)), cur, prev, cur, prev],
        out_specs=pl.BlockSpec((BLK, nq), lambda r, j: (j, 0)),
        out_shape=jax.ShapeDtypeStruct((NP, nq), BF16),
        compiler_params=_cp(("parallel", "arbitrary")),
        name="banded_c",
    )(sinks, qkv, qkv, qkv, qkv, qkv)


def _combine_kernel(o1, o2, o3, l1, l2, l3, out_ref):
    a, b, c = l1[...], l2[...], l3[...]
    m = jnp.maximum(jnp.maximum(a, b), c)
    ea, eb, ec = jnp.exp(a - m), jnp.exp(b - m), jnp.exp(c - m)
    tot = ea + eb + ec
    out_ref[...] = ((ea / tot) * o1[...] + (eb / tot) * o2[...] + (ec / tot) * o3[...]).astype(out_ref.dtype)


def _combine_groups(outs, lses):
    tm = 512
    spec = pl.BlockSpec((tm, A_W), lambda i: (i, 0))
    return pl.pallas_call(
        _combine_kernel,
        grid=(NP // tm,),
        in_specs=[spec] * 6,
        out_specs=spec,
        out_shape=jax.ShapeDtypeStruct((NP, A_W), BF16),
        compiler_params=_cp(("parallel",)),
        name="combine_groups",
    )(*outs, *lses)


def _dec_scores(q, nq, nk, kt, knew, dil, sink_col):
    rep = nq // nk
    w = kt.shape[1]
    qrows = jnp.concatenate([q[:, h * HD:(h + 1) * HD] for h in range(nq)], axis=0)
    qt = jnp.concatenate([qrows] * nk, axis=1) if nk > 1 else qrows
    r_i = lax.broadcasted_iota(I32, qt.shape, 0)
    l_i = lax.broadcasted_iota(I32, qt.shape, 1)
    qbd = jnp.where((r_i // (S * rep)) == (l_i // HD), qt, 0.0)
    scale = 1.0 / math.sqrt(HD)
    sc = _dot(qbd, kt) * scale
    tok = lax.broadcasted_iota(I32, sc.shape, 0) % S
    back = lax.broadcasted_iota(I32, sc.shape, 1) - tok
    sc = jnp.where((back >= 0) & ((back & (dil - 1)) == 0), sc, -jnp.inf)
    sn = _dot_nt(qbd, knew) * scale
    back = lax.broadcasted_iota(I32, sn.shape, 0) % S - lax.broadcasted_iota(I32, sn.shape, 1)
    sn = jnp.where((back >= 0) & ((back & (dil - 1)) == 0), sn, -jnp.inf)
    m = jnp.maximum(jnp.max(sc, axis=1, keepdims=True), jnp.max(sn, axis=1, keepdims=True))
    if sink_col is not None:
        m = jnp.maximum(m, sink_col)
    ec, en = jnp.exp(sc - m), jnp.exp(sn - m)
    den = jnp.sum(ec, axis=1, keepdims=True) + jnp.sum(en, axis=1, keepdims=True)
    if sink_col is not None:
        den = den + jnp.exp(sink_col - m)
    return ec, en, m, den


def _pick_heads(o_all, nq, nk):
    rep = nq // nk
    cols = []
    for h in range(nq):
        kh = h // rep
        cols.append(o_all[h * S:(h + 1) * S, kh * HD:(kh + 1) * HD])
    return jnp.concatenate(cols, axis=1)


def _shift_cache(new_ref, old, kv_new):
    rows, w = old.shape
    nblk = w // BLK
    keep = BLK - S
    lane = lax.broadcasted_iota(I32, (rows, BLK), 1)
    left = [pltpu.roll(old[:, j * BLK:(j + 1) * BLK], keep, axis=1) for j in range(nblk)]
    tail = jnp.concatenate([jnp.zeros((keep, rows), F32), kv_new], axis=0).T
    left.append(tail)
    shape = new_ref.shape[:-1] + (BLK,)
    for j in range(nblk):
        new_ref[:, :, :, j * BLK:(j + 1) * BLK] = jnp.where(lane < keep, left[j], left[j + 1]).reshape(shape)


def _decode_attn_kernel(sink_ref, qa_ref, qc_ref, c1_ref, c2_ref, c3_ref, cc_ref, *rest):
    n_alias = len(rest) - 6
    n1_ref, n2_ref, n3_ref, nc_ref, oa_ref, oc_ref = rest[n_alias:]
    qa = qa_ref[...]
    qc = qc_ref[...]
    caches = (c1_ref, c2_ref, c3_ref)
    news = (n1_ref, n2_ref, n3_ref)
    grp = []
    for gi, (w, dil) in enumerate(A_PATTERN):
        q = qa[:, gi * A_W:(gi + 1) * A_W]
        knew = qa[:, 768 + gi * A_W:768 + (gi + 1) * A_W]
        vnew = qa[:, 1536 + gi * A_W:1536 + (gi + 1) * A_W]
        old = caches[gi][...].reshape(2 * A_W, w)
        ec, en, m, den = _dec_scores(q, A_HEADS, A_HEADS, old[0:A_W], knew, dil, None)
        grp.append((ec, en, m + jnp.log(den), den, old[A_W:2 * A_W], vnew))
        _shift_cache(news[gi], old, jnp.concatenate([knew, vnew], axis=1))
    lmax = functools.reduce(jnp.maximum, [g[2] for g in grp])
    gws = [jnp.exp(g[2] - lmax) for g in grp]
    gtot = functools.reduce(jnp.add, gws)
    o_all = None
    for gi, (ec, en, _, den, vt, vnew) in enumerate(grp):
        wcol = (gws[gi] / gtot) / den
        t = _dot_nt(ec * wcol, vt) + _dot(en * wcol, vnew)
        o_all = t if o_all is None else o_all + t
    oa_ref[...] = _pick_heads(o_all, A_HEADS, A_HEADS).astype(oa_ref.dtype)
    nqc = C_HEADS * HD
    kvw = C_KV * HD
    old = cc_ref[...].reshape(2 * kvw, BLK)
    knew = qc[:, nqc:nqc + kvw]
    vnew = qc[:, nqc + kvw:nqc + 2 * kvw]
    ec, en, m, den = _dec_scores(qc[:, 0:nqc], C_HEADS, C_KV, old[0:kvw], knew, 1, sink_ref[...])
    o_c = _dot_nt(ec / den, old[kvw:2 * kvw]) + _dot(en / den, vnew)
    oc_ref[...] = _pick_heads(o_c, C_HEADS, C_KV).astype(oc_ref.dtype)
    _shift_cache(nc_ref, old, qc[:, nqc:nqc + 2 * kvw])


def _decode_attn(layer, sink_col, qkv_a, qkv_c, caches, prev_out):
    row0 = NP // S

    def cache_spec(c):
        return pl.BlockSpec((None, None) + c.shape[2:], lambda b: (layer, b, 0, 0, 0, 0))

    in_specs = [pl.BlockSpec((C_HEADS * S, 1), lambda b: (0, 0)),
                pl.BlockSpec((S, QKV_A), lambda b: (row0 + b, 0)),
                pl.BlockSpec((S, QKV_C), lambda b: (row0 + b, 0))] + [cache_spec(c) for c in caches]
    args = [sink_col, qkv_a, qkv_c] + list(caches)
    aliases = {}
    if prev_out is not None:
        for k, p in enumerate(prev_out):
            in_specs.append(pl.BlockSpec(memory_space=pl.ANY))
            args.append(p)
            aliases[len(args) - 1] = k
    out_specs = [cache_spec(c) for c in caches] + [
        pl.BlockSpec((S, A_W), lambda b: (b, 0)),
        pl.BlockSpec((S, C_HEADS * HD), lambda b: (b, 0))]
    out_shape = [jax.ShapeDtypeStruct(c.shape, F32) for c in caches] + [
        jax.ShapeDtypeStruct((NS, A_W), BF16), jax.ShapeDtypeStruct((NS, C_HEADS * HD), BF16)]
    return pl.pallas_call(
        _decode_attn_kernel,
        grid=(NSEQ,),
        in_specs=in_specs,
        out_specs=out_specs,
        out_shape=out_shape,
        input_output_aliases=aliases,
        compiler_params=_cp(("arbitrary",), 56),
        name=f"decode_attn{layer}",
    )(*args)


def _ssd_kernel(xbcdt_ref, z_ref, conv0_ref, h0_ref, convw_ref, convb_ref, dtb_ref, arow_ref,
                dskip_ref, nrm_ref, *rest, q_len):
    y_ref, hout_ref, xp_ref, h_ref, yacc_ref = rest[-5:]
    c = pl.program_id(1)
    Q = q_len

    @pl.when(c == 0)
    def _():
        xp_ref[0:8, :] = conv0_ref[...]
        h_ref[...] = h0_ref[...]

    xbc = xbcdt_ref[:, 0:CONV_DIM]
    xp_ref[8:8 + Q, :] = xbc
    w = convw_ref[...]
    acc = convb_ref[...] + w[3:4, :] * xbc
    acc = acc + w[2:3, :] * xp_ref[7:7 + Q, :]
    acc = acc + w[1:2, :] * xp_ref[6:6 + Q, :]
    acc = acc + w[0:1, :] * xp_ref[5:5 + Q, :]
    u = _silu(acc)
    xp_ref[0:8, :] = xbc[Q - 8:Q, :]

    dtp = _softplus(xbcdt_ref[:, CONV_DIM:XBCDT_W] + dtb_ref[...])
    da = dtp * arow_ref[...]
    ri = lax.broadcasted_iota(I32, (Q, Q), 0)
    ci = lax.broadcasted_iota(I32, (Q, Q), 1)
    tri = ri >= ci
    acs = jnp.dot(tri.astype(F32), da, precision=lax.Precision.HIGHEST, preferred_element_type=F32)
    acs_t = acs.T
    dt_t = dtp.T
    hpg = SSM_HEADS // 2
    for g in range(2):
        bg = u[:, SSM_INNER + g * SSM_N:SSM_INNER + (g + 1) * SSM_N]
        cg = u[:, SSM_INNER + (2 + g) * SSM_N:SSM_INNER + (3 + g) * SSM_N]
        cb = _dot_nt(cg, bg)
        for hh in range(hpg):
            hd = g * hpg + hh
            a_col = acs[:, hd:hd + 1]
            seg = a_col - acs_t[hd:hd + 1, :]
            decay = jnp.exp(jnp.where(tri, seg, -jnp.inf))
            mh = cb * decay * dt_t[hd:hd + 1, :]
            xh = u[:, hd * HD:(hd + 1) * HD]
            hst = h_ref[hd]
            y = _dot(mh, xh) + _dot_nt(cg, hst) * jnp.exp(a_col)
            a_last = acs[Q - 1:Q, hd:hd + 1]
            tail = jnp.exp(a_last - a_col) * dtp[:, hd:hd + 1]
            h_ref[hd] = hst * jnp.exp(a_last) + _dot_tn(xh * tail, bg)
            yacc_ref[:, hd * HD:(hd + 1) * HD] = y + dskip_ref[:, hd * HD:(hd + 1) * HD] * xh
    yz = yacc_ref[...] * _silu(z_ref[...])
    half = SSM_INNER // 2
    for g in range(2):
        yg = yz[:, g * half:(g + 1) * half]
        y_ref[:, g * half:(g + 1) * half] = (_rms(yg) * nrm_ref[:, g * half:(g + 1) * half]).astype(y_ref.dtype)

    @pl.when(c == pl.num_programs(1) - 1)
    def _():
        hout_ref[...] = h_ref[...]


def _ssd(layer, xbcdt, z, conv0, h0, convw, convb, dtb, arow, dskip, nrm, nseq, q_len, row0, h_prev=None):
    nchunks = (NP if nseq == 1 else S) // q_len
    rows = nseq * nchunks * q_len
    vec = lambda n: pl.BlockSpec((1, n), lambda b, c: (0, 0))
    state = pl.BlockSpec((None, None, SSM_HEADS, HD, SSM_N), lambda b, c: (layer, b, 0, 0, 0))
    in_specs = [pl.BlockSpec((q_len, XBCDT_W), lambda b, c: (row0 + b * nchunks + c, 0)),
                pl.BlockSpec((q_len, SSM_INNER), lambda b, c: (row0 + b * nchunks + c, 0)),
                pl.BlockSpec((None, 8, CONV_DIM), lambda b, c: (b, 0, 0)),
                state,
                pl.BlockSpec((4, CONV_DIM), lambda b, c: (0, 0)),
                vec(CONV_DIM), vec(128), vec(128), vec(SSM_INNER), vec(SSM_INNER)]
    args = [xbcdt, z, conv0, h0, convw, convb, dtb, arow, dskip, nrm]
    aliases = {}
    if h_prev is not None:
        in_specs.append(pl.BlockSpec(memory_space=pl.ANY))
        args.append(h_prev)
        aliases[len(args) - 1] = 1
    return pl.pallas_call(
        functools.partial(_ssd_kernel, q_len=q_len),
        grid=(nseq, nchunks),
        in_specs=in_specs,
        out_specs=[pl.BlockSpec((q_len, SSM_INNER), lambda b, c: (b * nchunks + c, 0)), state],
        out_shape=[jax.ShapeDtypeStruct((rows, SSM_INNER), BF16), jax.ShapeDtypeStruct(h0.shape, F32)],
        scratch_shapes=[pltpu.VMEM((q_len + 8, CONV_DIM), F32),
                        pltpu.VMEM((SSM_HEADS, HD, SSM_N), F32),
                        pltpu.VMEM((q_len, SSM_INNER), F32)],
        input_output_aliases=aliases,
        compiler_params=_cp(("arbitrary", "arbitrary")),
        name=f"ssd_q{q_len}",
    )(*args)


def _merge_kernel(h_ref, oa_ref, ob_ref, oc_ref, g0_ref, g1_ref, g2_ref, wa_ref, wb_ref, wc_ref, o_ref):
    h = h_ref[...]
    acc = _sigmoid(_dot(h, g0_ref[...])) * _dot(oa_ref[...], wa_ref[...])
    acc = acc + _sigmoid(_dot(h, g1_ref[...])) * _dot(ob_ref[...], wb_ref[...])
    acc = acc + _sigmoid(_dot(h, g2_ref[...])) * _dot(oc_ref[...], wc_ref[...])
    o_ref[...] = acc.astype(o_ref.dtype)


def _merge(layer, h1, oa, ob, oc, wgate, w_br_a, w_br_b, w_br_c):
    tm, tn = 1024, 512
    nj = D // tn
    row = lambda k: pl.BlockSpec((tm, k), lambda i, j: (i, 0))
    gate = lambda b: pl.BlockSpec((D, tn), lambda i, j: (0, b * nj + j))
    br = lambda k: pl.BlockSpec((None, k, tn), lambda i, j: (layer, 0, j))
    return pl.pallas_call(
        _merge_kernel,
        grid=(T // tm, nj),
        in_specs=[row(D), row(A_W), row(SSM_INNER), row(C_HEADS * HD), gate(0), gate(1), gate(2),
                  br(A_W), br(SSM_INNER), br(C_HEADS * HD)],
        out_specs=pl.BlockSpec((tm, tn), lambda i, j: (i, j)),
        out_shape=jax.ShapeDtypeStruct((T, D), BF16),
        compiler_params=_cp(("parallel", "arbitrary"), 56),
        name="merge",
    )(h1, oa, ob, oc, wgate, wgate, wgate, w_br_a, w_br_b, w_br_c)


def _outproj_kernel(m_ref, w_ref, x_ref, gpost_ref, gpre_ref, gap_ref, gas_ref, shp_ref, shs_ref,
                    scp_ref, scs_ref, x_out_ref, hf_ref, hp_ref):
    is_p = pl.program_id(0) < NPT_TOK
    ga = _sel_tile(is_p, gap_ref, gas_ref)
    sh = _sel_tile(is_p, shp_ref, shs_ref)
    sc = _sel_tile(is_p, scp_ref, scs_ref)
    mix = _dot(m_ref[...], w_ref[...])
    xn = x_ref[...] + ga * (_rms(mix) * gpost_ref[...])
    x_out_ref[...] = xn
    hf = (_rms(xn) * gpre_ref[...]) * (1.0 + sc) + sh
    hf_ref[...] = hf
    hp_ref[...] = _pack_bf16_pairs(hf)


def _outproj(merged, w_out_bf, x, g_post, g_pre, mod_p, mod_s):
    tok = pl.BlockSpec((TM_TOK, D), lambda i: (i, 0))
    vec = pl.BlockSpec((1, D), lambda i: (0, 0))
    return pl.pallas_call(
        _outproj_kernel,
        grid=(T // TM_TOK,),
        in_specs=[tok, pl.BlockSpec((D, D), lambda i: (0, 0)), tok, vec, vec]
        + _mod_specs(2) + _mod_specs(3) + _mod_specs(4),
        out_specs=[tok, tok, pl.BlockSpec((TM_TOK, D // 2), lambda i: (i, 0))],
        out_shape=[jax.ShapeDtypeStruct((T, D), F32)] * 2 + [jax.ShapeDtypeStruct((T, D // 2), jnp.uint32)],
        compiler_params=_cp(("parallel",), 56),
        name="outproj",
    )(merged, w_out_bf, x, g_post.reshape(1, D), g_pre.reshape(1, D), mod_p, mod_s, mod_p, mod_s, mod_p, mod_s)


def _router_kernel(hf_ref, rwt_ref, bias_ref, topi_ref, topw_ref):
    tm = hf_ref.shape[0]
    logits = lax.dot_general(rwt_ref[...], hf_ref[...], (((1,), (1,)), ((), ())),
                             precision=lax.Precision.HIGHEST, preferred_element_type=F32)
    scores = _sigmoid(logits)
    sel = scores + bias_ref[...]
    ng = 8
    sel3 = sel.reshape(ng, N_EXP // ng, tm)
    idx3 = lax.broadcasted_iota(I32, sel3.shape, 1)
    m1 = jnp.max(sel3, axis=1, keepdims=True)
    first = jnp.min(jnp.where(sel3 == m1, idx3, N_EXP), axis=1, keepdims=True)
    m2 = jnp.max(jnp.where(idx3 == first, -jnp.inf, sel3), axis=1, keepdims=True)
    gs = m1 + m2
    gidx = lax.broadcasted_iota(I32, (ng, 1, tm), 0)
    gmask = jnp.zeros((ng, 1, tm), jnp.bool_)
    for _ in range(4):
        m = jnp.max(gs, axis=0, keepdims=True)
        f = jnp.min(jnp.where(gs == m, gidx, ng), axis=0, keepdims=True)
        hit = gidx == f
        gmask = jnp.logical_or(gmask, hit)
        gs = jnp.where(hit, -jnp.inf, gs)
    masked = jnp.where(gmask, sel3, -jnp.inf).reshape(N_EXP, tm)
    eidx = lax.broadcasted_iota(I32, (N_EXP, tm), 0)
    tis, tws = [], []
    for _ in range(TOP_K):
        m = jnp.max(masked, axis=0, keepdims=True)
        f = jnp.min(jnp.where(masked == m, eidx, N_EXP), axis=0, keepdims=True)
        hit = eidx == f
        tis.append(f)
        tws.append(jnp.sum(jnp.where(hit, scores, 0.0), axis=0, keepdims=True))
        masked = jnp.where(hit, -jnp.inf, masked)
    wsum = functools.reduce(jnp.add, tws)
    topi_ref[...] = jnp.concatenate(tis, axis=0)
    topw_ref[...] = jnp.concatenate([w / wsum * ROUTED_SCALE for w in tws], axis=0)


def _router(hf, rwt, bias_col):
    tm = 512
    return pl.pallas_call(
        _router_kernel,
        grid=(T // tm,),
        in_specs=[pl.BlockSpec((tm, D), lambda i: (i, 0)),
                  pl.BlockSpec((N_EXP, D), lambda i: (0, 0)),
                  pl.BlockSpec((N_EXP, 1), lambda i: (0, 0))],
        out_specs=[pl.BlockSpec((TOP_K, tm), lambda i: (0, i))] * 2,
        out_shape=[jax.ShapeDtypeStruct((TOP_K, T), I32), jax.ShapeDtypeStruct((TOP_K, T), F32)],
        compiler_params=_cp(("parallel",)),
        name="router",
    )(hf, rwt, bias_col)


def _route_tables(topi):
    onehot = topi[None] == jnp.arange(N_EXP, dtype=I32)[:, None, None]
    tok = jnp.any(onehot, axis=1).astype(I32)
    incl = jnp.cumsum(tok, axis=1)
    counts = incl[:, -1]
    padded = (counts + BM - 1) // BM * BM
    pad_end = jnp.cumsum(padded)
    base = (pad_end - padded)[:, None] + incl - tok
    dest = jnp.sum(jnp.where(onehot, base[:, None, :], 0), axis=0).astype(I32)
    starts = jnp.arange(NB, dtype=I32) * BM
    block_e = jnp.minimum(jnp.sum(pad_end[None, :] <= starts[:, None], axis=1), N_EXP - 1).astype(I32)
    n_used = (pad_end[-1] // BM).astype(I32).reshape(1)
    return dest, block_e, n_used


def _dispatch_kernel(dest_ref, hf_ref, xs_in_ref, xs_ref, sem):
    del xs_in_ref

    def body(t, carry):
        for k in range(TOP_K):
            pltpu.make_async_copy(hf_ref.at[pl.ds(t, 1)], xs_ref.at[pl.ds(dest_ref[k, t], 1)], sem).start()
        return carry

    lax.fori_loop(0, TT_DISP, body, 0)
    for k in range(TOP_K):
        pltpu.make_async_copy(hf_ref, xs_ref.at[pl.ds(0, TT_DISP)], sem).wait()


def _dispatch(dest, hf, xs_buf):
    return pl.pallas_call(
        _dispatch_kernel,
        grid=(T // TT_DISP,),
        in_specs=[pl.BlockSpec((TOP_K, TT_DISP), lambda i: (0, i), memory_space=pltpu.SMEM),
                  pl.BlockSpec((TT_DISP, D // 2), lambda i: (i, 0)),
                  pl.BlockSpec(memory_space=pl.ANY)],
        out_specs=pl.BlockSpec(memory_space=pl.ANY),
        out_shape=jax.ShapeDtypeStruct((P_ROWS, D // 2), jnp.uint32),
        scratch_shapes=[pltpu.SemaphoreType.DMA(())],
        input_output_aliases={2: 0},
        compiler_params=_cp(("arbitrary",)),
        name="moe_dispatch",
    )(dest, hf, xs_buf)


def _swiglu(x, wg, wu, wd):
    return _dot(_silu(_dot(x, wg)) * _dot(x, wu), wd)


def _expert_kernel(be_ref, nu_ref, x_ref, wg_ref, wu_ref, wd_ref, o_ref):
    del be_ref
    used = pl.program_id(0) < nu_ref[0]

    @pl.when(used)
    def _():
        o_ref[...] = _swiglu(_unpack_bf16_pairs(x_ref[...]), wg_ref[...], wu_ref[...], wd_ref[...])

    @pl.when(jnp.logical_not(used))
    def _():
        o_ref[...] = jnp.zeros_like(o_ref)


def _experts(layer, xs, block_e, n_used, exp_gate, exp_up, exp_down):
    wspec = lambda a, b: pl.BlockSpec((None, None, a, b), lambda i, be, nu: (layer, be[i], 0, 0))
    return pl.pallas_call(
        _expert_kernel,
        grid_spec=pltpu.PrefetchScalarGridSpec(
            num_scalar_prefetch=2,
            grid=(NB,),
            in_specs=[pl.BlockSpec((BM, D // 2), lambda i, be, nu: (i, 0)),
                      wspec(D, MOE_H), wspec(D, MOE_H), wspec(MOE_H, D)],
            out_specs=pl.BlockSpec((BM, D), lambda i, be, nu: (i, 0))),
        out_shape=jax.ShapeDtypeStruct((P_ROWS, D), F32),
        compiler_params=_cp(("arbitrary",), 56),
        name="moe_experts",
    )(block_e, n_used, xs, exp_gate, exp_up, exp_down)


def _shared_kernel(x_ref, wg_ref, wu_ref, wd_ref, o_ref):
    o_ref[...] = _swiglu(x_ref[...], wg_ref[...], wu_ref[...], wd_ref[...])


def _shared_expert(layer, hf, wg, wu, wd):
    tm = 512
    wspec = lambda a, b: pl.BlockSpec((None, a, b), lambda i: (layer, 0, 0))
    return pl.pallas_call(
        _shared_kernel,
        grid=(T // tm,),
        in_specs=[pl.BlockSpec((tm, D), lambda i: (i, 0)), wspec(D, MOE_H), wspec(D, MOE_H), wspec(MOE_H, D)],
        out_specs=pl.BlockSpec((tm, D), lambda i: (i, 0)),
        out_shape=jax.ShapeDtypeStruct((T, D), F32),
        compiler_params=_cp(("parallel",), 56),
        name="moe_shared",
    )(hf, wg, wu, wd)


def _moe_combine_kernel(dest_ref, dnext_ref, ys_ref, w_ref, fsh_ref, x_ref, gpost_ref, gap_ref, gas_ref,
                        o_ref, gbuf, sems):
    tt = TT_DISP
    i = pl.program_id(0)
    is_p = i < NP // tt
    slot = i % 2

    def gather(d_ref, s):
        def body(t, carry):
            for k in range(TOP_K):
                pltpu.make_async_copy(ys_ref.at[pl.ds(d_ref[k, t], 1)], gbuf.at[s, k, pl.ds(t, 1)],
                                      sems.at[s]).start()
            return carry
        lax.fori_loop(0, tt, body, 0)

    @pl.when(i == 0)
    def _():
        gather(dest_ref, 0)

    @pl.when(i + 1 < pl.num_programs(0))
    def _():
        gather(dnext_ref, 1 - slot)

    for k in range(TOP_K):
        pltpu.make_async_copy(ys_ref.at[pl.ds(0, tt)], gbuf.at[slot, k], sems.at[slot]).wait()
    w = w_ref[...]
    f = fsh_ref[...]
    for k in range(TOP_K):
        f = f + gbuf[slot, k] * w[:, k:k + 1]
    ga = jnp.where(is_p, gap_ref[0:1, :], gas_ref[...])
    o_ref[...] = x_ref[...] + ga * (_rms(f) * gpost_ref[...])


def _moe_combine(dest, ys, topw_t, f_shared, x, g_post, mod_p, mod_s):
    tt = TT_DISP
    tok = pl.BlockSpec((tt, D), lambda i: (i, 0))
    return pl.pallas_call(
        _moe_combine_kernel,
        grid=(T // tt,),
        in_specs=[pl.BlockSpec((TOP_K, tt), lambda i: (0, i), memory_space=pltpu.SMEM),
                  pl.BlockSpec((TOP_K, tt), lambda i: (0, jnp.minimum(i + 1, T // tt - 1)),
                               memory_space=pltpu.SMEM),
                  pl.BlockSpec(memory_space=pl.ANY),
                  pl.BlockSpec((tt, TOP_K), lambda i: (i, 0)),
                  tok, tok, pl.BlockSpec((1, D), lambda i: (0, 0)),
                  pl.BlockSpec((8, D), lambda i: (0, 5)),
                  pl.BlockSpec((tt, D), lambda i: (jnp.maximum(i - NP // tt, 0), 5))],
        out_specs=tok,
        out_shape=jax.ShapeDtypeStruct((T, D), F32),
        scratch_shapes=[pltpu.VMEM((2, TOP_K, tt, D), F32), pltpu.SemaphoreType.DMA((2,))],
        compiler_params=_cp(("arbitrary",)),
        name="moe_combine",
    )(dest, dest, ys, topw_t, f_shared, x, g_post.reshape(1, D), mod_p, mod_s)


def _pad_cols(w, n):
    return jnp.pad(w, ((0, 0), (0, n - w.shape[1])))


def _layer(l, x, mod_p, mod_s, p, dec_caches, dec_prev, ssm_prev, xs_buf):
    w_in = p['w_in'][l]
    h1 = _prenorm(x, p['norm_pre_mix'][l], mod_p, mod_s)
    qkv_a = _matmul(h1, w_in[:, OFF_QKV_A:OFF_QKV_C].astype(BF16), 1024, 768, "proj_qkv_a")
    qkv_c = _matmul(h1, w_in[:, OFF_QKV_C:OFF_Z].astype(BF16), 1024, QKV_C, "proj_qkv_c")
    z = _matmul(h1, w_in[:, OFF_Z:OFF_XBC].astype(BF16), 1024, SSM_INNER, "proj_z")
    xbcdt = _matmul(h1, _pad_cols(w_in[:, OFF_XBC:OFF_GATES], XBCDT_W).astype(BF16), 1024, XBCDT_W, "proj_xbcdt")

    outs, lses = zip(*[_banded_a(qkv_a, gi, dil) for gi, (_, dil) in enumerate(A_PATTERN)])
    oa_p = _combine_groups(outs, lses)
    oc_p = _banded_c(qkv_c, p['sinks'][l])
    sink_col = jnp.repeat(p['sinks'][l], S).reshape(C_HEADS * S, 1)
    n1, n2, n3, ncc, oa_s, oc_s = _decode_attn(l, sink_col, qkv_a, qkv_c, dec_caches, dec_prev)

    dtb = jnp.pad(p['dt_bias'][l], (0, 128 - SSM_HEADS)).reshape(1, 128)
    arow = jnp.pad(-jnp.exp(p['a_log'][l]), (0, 128 - SSM_HEADS)).reshape(1, 128)
    dskip = jnp.repeat(p['d_skip'][l], HD).reshape(1, SSM_INNER)
    nrm = p['ssm_norm'][l].reshape(1, SSM_INNER)
    convw = p['conv_w'][l]
    convb = p['conv_b'][l].reshape(1, CONV_DIM)
    ob_p, ssm_p = _ssd(0, xbcdt, z, jnp.zeros((1, 8, CONV_DIM), F32),
                       jnp.zeros((1, 1, SSM_HEADS, HD, SSM_N), F32),
                       convw, convb, dtb, arow, dskip, nrm, 1, BLK, 0)
    conv0_s = jnp.pad(p['state_conv'][l], ((0, 0), (5, 0), (0, 0)))
    ob_s, ssm_s = _ssd(l, xbcdt, z, conv0_s, p['state_ssm'],
                       convw, convb, dtb, arow, dskip, nrm, NSEQ, S, NP // S, h_prev=ssm_prev)

    oa = jnp.concatenate([oa_p, oa_s], axis=0)
    ob = jnp.concatenate([ob_p, ob_s], axis=0)
    oc = jnp.concatenate([oc_p, oc_s], axis=0)
    merged = _merge(l, h1, oa, ob, oc, w_in[:, OFF_GATES:].astype(BF16), p['w_br_a'], p['w_br_b'], p['w_br_c'])
    x, hf, hf_packed = _outproj(merged, p['w_out'][l].astype(BF16), x, p['norm_post_mix'][l],
                                p['norm_pre_ffn'][l], mod_p, mod_s)

    topi, topw = _router(hf, p['router_w'][l].T, p['router_bias'][l].reshape(N_EXP, 1))
    dest, block_e, n_used = _route_tables(topi)
    xs = _dispatch(dest, hf_packed, xs_buf)
    ys = _experts(l, xs, block_e, n_used, p['exp_gate'], p['exp_up'], p['exp_down'])
    f_shared = _shared_expert(l, hf, p['shared_gate'], p['shared_up'], p['shared_down'])
    x = _moe_combine(dest, ys, topw.T, f_shared, x, p['norm_post_ffn'][l], mod_p, mod_s)

    def last_kv(src, k_off, v_off, rows, heads):
        k = src[NP - rows:NP, k_off:k_off + heads * HD].reshape(rows, heads, HD)
        v = src[NP - rows:NP, v_off:v_off + heads * HD].reshape(rows, heads, HD)
        return jnp.stack([k, v], axis=1)[None]

    a_states = [last_kv(qkv_a, 768 + gi * A_W, 1536 + gi * A_W, w, A_HEADS)
                for gi, (w, _) in enumerate(A_PATTERN)]
    c_state = last_kv(qkv_c, C_HEADS * HD, C_HEADS * HD + C_KV * HD, BLK, C_KV)
    conv_p = xbcdt[NP - 3:NP, :CONV_DIM][None]
    conv_s = xbcdt[NP:, :CONV_DIM].reshape(NSEQ, S, CONV_DIM)[:, S - 3:]
    prompt_state = (*a_states, c_state, ssm_p.reshape(1, 2, SSM_HEADS // 2, HD, SSM_N), conv_p)
    return x, prompt_state, (n1, n2, n3, ncc), ssm_s, conv_s, xs


def kernel(x_prompt, x_sample, cache_a1_kv, cache_a2_kv, cache_a3_kv, cache_c_kv, state_ssm, state_conv,
           c_prompt, c_sample, ada_w, ada_b, norm_pre_mix, norm_post_mix, norm_pre_ffn, norm_post_ffn,
           w_in, conv_w, conv_b, dt_bias, a_log, d_skip, ssm_norm, sinks, w_br_a, w_br_b, w_br_c, w_out,
           router_w, router_bias, exp_gate, exp_up, exp_down, shared_gate, shared_up, shared_down):
    p = dict(norm_pre_mix=norm_pre_mix, norm_post_mix=norm_post_mix, norm_pre_ffn=norm_pre_ffn,
             norm_post_ffn=norm_post_ffn, w_in=w_in, conv_w=conv_w, conv_b=conv_b, dt_bias=dt_bias,
             a_log=a_log, d_skip=d_skip, ssm_norm=ssm_norm, sinks=sinks, w_br_a=w_br_a, w_br_b=w_br_b,
             w_br_c=w_br_c, w_out=w_out, router_w=router_w, router_bias=router_bias, exp_gate=exp_gate,
             exp_up=exp_up, exp_down=exp_down, shared_gate=shared_gate, shared_up=shared_up,
             shared_down=shared_down, state_conv=state_conv,
             state_ssm=state_ssm.reshape(DEPTH, NSEQ, SSM_HEADS, HD, SSM_N))
    x = jnp.concatenate([x_prompt.reshape(NP, D), x_sample.reshape(NS, D)], axis=0)
    c_all = jnp.concatenate([c_prompt, c_sample, jnp.zeros((7, D), F32)], axis=0)
    mod = _ada(c_all, ada_w, ada_b)
    dec_caches = tuple(jnp.transpose(c, (0, 1, 3, 4, 5, 2))
                       for c in (cache_a1_kv, cache_a2_kv, cache_a3_kv, cache_c_kv))
    xs_buf = jnp.zeros((P_ROWS, D // 2), jnp.uint32)
    dec_prev = ssm_s = None
    prompt_states, conv_s = [], []
    for l in range(DEPTH):
        mod_s = jnp.repeat(mod[l, 1:1 + NSEQ], S, axis=0)
        x, pst, dec_prev, ssm_s, conv_l, xs_buf = _layer(l, x, mod[l], mod_s, p, dec_caches, dec_prev,
                                                          ssm_s, xs_buf)
        prompt_states.append(pst)
        conv_s.append(conv_l)
    a1_p, a2_p, a3_p, c_p, ssm_p, conv_p = (jnp.stack(t) for t in zip(*prompt_states))
    n1, n2, n3, ncc = (jnp.transpose(c, (0, 1, 5, 2, 3, 4)) for c in dec_prev)
    y_p = x[:NP].reshape(1, NP, D)
    y_s = x[NP:].reshape(NSEQ, S, D)
    return (y_p, y_s, a1_p, n1, a2_p, n2, a3_p, n3, c_p, ncc,
            ssm_p, ssm_s.reshape(state_ssm.shape), conv_p, jnp.stack(conv_s))
```

```python
import functools
import math

import jax
import jax.numpy as jnp
from jax import lax
from jax.experimental import pallas as pl
from jax.experimental.pallas import tpu as pltpu

F32 = jnp.float32
BF16 = jnp.bfloat16
I32 = jnp.int32

D = 2048
NP = 8192
NSEQ = 128
S = 8
NS = NSEQ * S
T = NP + NS
DEPTH = 2
HD = 64
EPS = 1e-6
A_PATTERN = ((128, 1), (512, 4), (2048, 16))
A_HEADS = 4
A_W = 256
QKV_A = 2304
C_HEADS = 12
C_KV = 3
QKV_C = 1152
SSM_INNER = 1024
SSM_HEADS = 16
SSM_N = 128
CONV_DIM = 1536
XBCDT_W = 1664
N_EXP = 64
TOP_K = 8
MOE_H = 512
ROUTED_SCALE = 2.5
BLK = 128

OFF_QKV_A = 0
OFF_QKV_C = 2304
OFF_Z = 3456
OFF_XBC = 4480
OFF_DT = 6016
OFF_GATES = 6032

TM_TOK = 256
NPT_TOK = NP // TM_TOK
BM = 256
NB = T * TOP_K // BM + N_EXP
P_ROWS = NB * BM
TT_DISP = 128
LANES = 128
XS_R = D // 2 // LANES


def _cp(sem, vmem_mb=48):
    return pltpu.CompilerParams(dimension_semantics=sem, vmem_limit_bytes=vmem_mb << 20)


def _sigmoid(x):
    return 1.0 / (1.0 + jnp.exp(-x))


def _silu(x):
    return x * _sigmoid(x)


def _softplus(x):
    return jnp.maximum(x, 0.0) + jnp.log1p(jnp.exp(-jnp.abs(x)))


def _rms(x):
    return x * lax.rsqrt(jnp.mean(x * x, axis=-1, keepdims=True) + EPS)


def _dot(a, b):
    return jnp.dot(a.astype(BF16), b.astype(BF16), preferred_element_type=F32)


def _dot_nt(a, b):
    return lax.dot_general(a.astype(BF16), b.astype(BF16), (((1,), (1,)), ((), ())),
                           preferred_element_type=F32)


def _dot_tn(a, b):
    return lax.dot_general(a.astype(BF16), b.astype(BF16), (((0,), (0,)), ((), ())),
                           preferred_element_type=F32)


def _pack_bf16_pairs(x):
    half = x.shape[1] // 2
    bits = lax.bitcast_convert_type(x.astype(BF16).astype(F32), jnp.uint32)
    return (bits[:, :half] >> 16) | (bits[:, half:] & jnp.uint32(0xFFFF0000))


def _unpack_bf16_pairs(w):
    lo = lax.bitcast_convert_type(w << 16, F32)
    hi = lax.bitcast_convert_type(w & jnp.uint32(0xFFFF0000), F32)
    return jnp.concatenate([lo.astype(BF16), hi.astype(BF16)], axis=1)


def _store_token_tiles(ref, val):
    m, c = val.shape[0], val.shape[1] // LANES
    for j in range(c):
        ref[pl.ds(j, m, stride=c), :] = val[:, j * LANES:(j + 1) * LANES]


def _load_token_tiles(ref, lead, m, c):
    return jnp.concatenate([ref[(*lead, pl.ds(j, m, stride=c), slice(None))] for j in range(c)], axis=1)


def _sel_tile(is_prompt, p_ref, s_ref):
    return jnp.where(is_prompt, p_ref[0:1, :], s_ref[...])


def _ada_kernel(c_ref, w_ref, b_ref, o_ref):
    o_ref[...] = _dot(_silu(c_ref[...]), w_ref[...]) + b_ref[...]


def _ada(c_all, ada_w, ada_b):
    rows = c_all.shape[0]
    tn = 1024
    return pl.pallas_call(
        _ada_kernel,
        grid=(DEPTH, 6 * D // tn),
        in_specs=[pl.BlockSpec((rows, D), lambda l, j: (0, 0)),
                  pl.BlockSpec((None, D, tn), lambda l, j: (l, 0, j)),
                  pl.BlockSpec((None, 1, tn), lambda l, j: (l, 0, j))],
        out_specs=pl.BlockSpec((None, rows, tn), lambda l, j: (l, 0, j)),
        out_shape=jax.ShapeDtypeStruct((DEPTH, rows, 6 * D), F32),
        compiler_params=_cp(("arbitrary", "arbitrary")),
        name="ada",
    )(c_all, ada_w, ada_b.reshape(DEPTH, 1, 6 * D))


def _mod_specs(col):
    return [pl.BlockSpec((8, D), lambda i: (0, col)),
            pl.BlockSpec((TM_TOK, D), lambda i: (jnp.maximum(i - NPT_TOK, 0), col))]


def _prenorm_kernel(x_ref, g_ref, shp_ref, shs_ref, scp_ref, scs_ref, o_ref):
    is_p = pl.program_id(0) < NPT_TOK
    sh = _sel_tile(is_p, shp_ref, shs_ref)
    sc = _sel_tile(is_p, scp_ref, scs_ref)
    y = _rms(x_ref[...]) * g_ref[...]
    o_ref[...] = (y * (1.0 + sc) + sh).astype(o_ref.dtype)


def _prenorm(x, g, mod_p, mod_s):
    return pl.pallas_call(
        _prenorm_kernel,
        grid=(T // TM_TOK,),
        in_specs=[pl.BlockSpec((TM_TOK, D), lambda i: (i, 0)),
                  pl.BlockSpec((1, D), lambda i: (0, 0))] + _mod_specs(0) + _mod_specs(1),
        out_specs=pl.BlockSpec((TM_TOK, D), lambda i: (i, 0)),
        out_shape=jax.ShapeDtypeStruct((T, D), BF16),
        compiler_params=_cp(("parallel",)),
        name="prenorm",
    )(x, g.reshape(1, D), mod_p, mod_s, mod_p, mod_s)


def _mm_kernel(x_ref, w_ref, o_ref):
    o_ref[...] = _dot(x_ref[...], w_ref[...]).astype(o_ref.dtype)


def _matmul(x, w, tm, tn, name):
    m, k = x.shape
    n = w.shape[1]
    return pl.pallas_call(
        _mm_kernel,
        grid=(m // tm, n // tn),
        in_specs=[pl.BlockSpec((tm, k), lambda i, j: (i, 0)),
                  pl.BlockSpec((k, tn), lambda i, j: (0, j))],
        out_specs=pl.BlockSpec((tm, tn), lambda i, j: (i, j)),
        out_shape=jax.ShapeDtypeStruct((m, n), F32),
        compiler_params=_cp(("parallel", "arbitrary")),
        name=name,
    )(x, w)


def _banded_kernel(*refs, nh, k_off, v_off, with_sink, with_lse):
    refs = list(refs)
    sink_ref = refs.pop(0) if with_sink else None
    q_ref, kc_ref, kp_ref, vc_ref, vp_ref = refs[:5]
    o_ref = refs[5]
    lse_ref = refs[6] if with_lse else None
    has_prev = pl.program_id(1) > 0
    row = lax.broadcasted_iota(I32, (BLK, 2 * BLK), 0)
    col = lax.broadcasted_iota(I32, (BLK, 2 * BLK), 1)
    dist = row + BLK - col
    valid = (dist >= 0) & (dist <= BLK) & (has_prev | (col >= BLK))
    q = q_ref[...]
    kc, kp, vc, vp = kc_ref[...], kp_ref[...], vc_ref[...], vp_ref[...]
    for h in range(nh):
        qh = q[:, h * HD:(h + 1) * HD]
        ko, vo = k_off[h], v_off[h]
        kk = jnp.concatenate([kp[:, ko:ko + HD], kc[:, ko:ko + HD]], axis=0)
        vv = jnp.concatenate([vp[:, vo:vo + HD], vc[:, vo:vo + HD]], axis=0)
        s = _dot_nt(qh, kk) * (1.0 / math.sqrt(HD))
        s = jnp.where(valid, s, -jnp.inf)
        m = jnp.max(s, axis=-1, keepdims=True)
        if with_sink:
            sk = sink_ref[h]
            m = jnp.maximum(m, sk)
        e = jnp.exp(s - m)
        den = jnp.sum(e, axis=-1, keepdims=True)
        if with_sink:
            den = den + jnp.exp(sk - m)
        o = _dot(e / den, vv)
        o_ref[:, h * HD:(h + 1) * HD] = o.astype(o_ref.dtype)
        if with_lse:
            lse_ref[:, h * HD:(h + 1) * HD] = jnp.broadcast_to(m + jnp.log(den), (BLK, HD))


def _banded_a(qkv, gi, dil):
    cols = jnp.concatenate([qkv[:NP, o + gi * A_W:o + (gi + 1) * A_W] for o in (0, 768, 1536)], axis=1)
    view = cols.astype(BF16).reshape(NP // dil, dil * 3 * A_W)
    nb = NP // dil // BLK
    cpr = 3

    def spec(off, prev):
        off = off // 3
        if prev:
            return pl.BlockSpec((BLK, A_W), lambda r, j: (jnp.maximum(j - 1, 0), r * cpr + off))
        return pl.BlockSpec((BLK, A_W), lambda r, j: (j, r * cpr + off))

    offs = tuple(h * HD for h in range(A_HEADS))
    out_spec = pl.BlockSpec((BLK, A_W), lambda r, j: (j, r))
    o, lse = pl.pallas_call(
        functools.partial(_banded_kernel, nh=A_HEADS, k_off=offs, v_off=offs,
                          with_sink=False, with_lse=True),
        grid=(dil, nb),
        in_specs=[spec(0, False), spec(3, False), spec(3, True), spec(6, False), spec(6, True)],
        out_specs=[out_spec, out_spec],
        out_shape=[jax.ShapeDtypeStruct((NP // dil, dil * A_W), F32)] * 2,
        compiler_params=_cp(("parallel", "arbitrary")),
        name=f"banded_a{gi}",
    )(view, view, view, view, view)
    return o.reshape(NP, A_W), lse.reshape(NP, A_W)


def _banded_c(qkv, sinks):
    nq = C_HEADS * HD
    kvw = 2 * C_KV * HD
    k_off = tuple((h // 4) * HD for h in range(C_HEADS))
    v_off = tuple(C_KV * HD + (h // 4) * HD for h in range(C_HEADS))
    cur = pl.BlockSpec((BLK, kvw), lambda r, j: (j, nq // kvw))
    prev = pl.BlockSpec((BLK, kvw), lambda r, j: (jnp.maximum(j - 1, 0), nq // kvw))
    return pl.pallas_call(
        functools.partial(_banded_kernel, nh=C_HEADS, k_off=k_off, v_off=v_off,
                          with_sink=True, with_lse=False),
        grid=(1, NP // BLK),
        in_specs=[pl.BlockSpec(memory_space=pltpu.SMEM),
                  pl.BlockSpec((BLK, nq), lambda r, j: (j, 0)), cur, prev, cur, prev],
        out_specs=pl.BlockSpec((BLK, nq), lambda r, j: (j, 0)),
        out_shape=jax.ShapeDtypeStruct((NP, nq), BF16),
        compiler_params=_cp(("parallel", "arbitrary")),
        name="banded_c",
    )(sinks, qkv, qkv, qkv, qkv, qkv)


def _combine_kernel(o1, o2, o3, l1, l2, l3, out_ref):
    a, b, c = l1[...], l2[...], l3[...]
    m = jnp.maximum(jnp.maximum(a, b), c)
    ea, eb, ec = jnp.exp(a - m), jnp.exp(b - m), jnp.exp(c - m)
    tot = ea + eb + ec
    out_ref[...] = ((ea / tot) * o1[...] + (eb / tot) * o2[...] + (ec / tot) * o3[...]).astype(out_ref.dtype)


def _combine_groups(outs, lses):
    tm = 512
    spec = pl.BlockSpec((tm, A_W), lambda i: (i, 0))
    return pl.pallas_call(
        _combine_kernel,
        grid=(NP // tm,),
        in_specs=[spec] * 6,
        out_specs=spec,
        out_shape=jax.ShapeDtypeStruct((NP, A_W), BF16),
        compiler_params=_cp(("parallel",)),
        name="combine_groups",
    )(*outs, *lses)


def _dec_scores(q, nq, nk, kt, knew, dil, sink_col):
    rep = nq // nk
    w = kt.shape[1]
    qrows = jnp.concatenate([q[:, h * HD:(h + 1) * HD] for h in range(nq)], axis=0)
    qt = jnp.concatenate([qrows] * nk, axis=1) if nk > 1 else qrows
    r_i = lax.broadcasted_iota(I32, qt.shape, 0)
    l_i = lax.broadcasted_iota(I32, qt.shape, 1)
    qbd = jnp.where((r_i // (S * rep)) == (l_i // HD), qt, 0.0)
    scale = 1.0 / math.sqrt(HD)
    sc = _dot(qbd, kt) * scale
    tok = lax.broadcasted_iota(I32, sc.shape, 0) % S
    back = lax.broadcasted_iota(I32, sc.shape, 1) - tok
    sc = jnp.where((back >= 0) & ((back & (dil - 1)) == 0), sc, -jnp.inf)
    sn = _dot_nt(qbd, knew) * scale
    back = lax.broadcasted_iota(I32, sn.shape, 0) % S - lax.broadcasted_iota(I32, sn.shape, 1)
    sn = jnp.where((back >= 0) & ((back & (dil - 1)) == 0), sn, -jnp.inf)
    m = jnp.maximum(jnp.max(sc, axis=1, keepdims=True), jnp.max(sn, axis=1, keepdims=True))
    if sink_col is not None:
        m = jnp.maximum(m, sink_col)
    ec, en = jnp.exp(sc - m), jnp.exp(sn - m)
    den = jnp.sum(ec, axis=1, keepdims=True) + jnp.sum(en, axis=1, keepdims=True)
    if sink_col is not None:
        den = den + jnp.exp(sink_col - m)
    return ec, en, m, den


def _pick_heads(o_all, nq, nk):
    rep = nq // nk
    cols = []
    for h in range(nq):
        kh = h // rep
        cols.append(o_all[h * S:(h + 1) * S, kh * HD:(kh + 1) * HD])
    return jnp.concatenate(cols, axis=1)


def _shift_cache(new_ref, old, kv_new):
    rows, w = old.shape
    nblk = w // BLK
    keep = BLK - S
    lane = lax.broadcasted_iota(I32, (rows, BLK), 1)
    left = [pltpu.roll(old[:, j * BLK:(j + 1) * BLK], keep, axis=1) for j in range(nblk)]
    tail = jnp.concatenate([jnp.zeros((keep, rows), F32), kv_new], axis=0).T
    left.append(tail)
    shape = new_ref.shape[:-1] + (BLK,)
    for j in range(nblk):
        new_ref[:, :, :, j * BLK:(j + 1) * BLK] = jnp.where(lane < keep, left[j], left[j + 1]).reshape(shape)


def _decode_attn_kernel(sink_ref, qa_ref, qc_ref, c1_ref, c2_ref, c3_ref, cc_ref, *rest):
    n_alias = len(rest) - 6
    n1_ref, n2_ref, n3_ref, nc_ref, oa_ref, oc_ref = rest[n_alias:]
    qa = qa_ref[...]
    qc = qc_ref[...]
    caches = (c1_ref, c2_ref, c3_ref)
    news = (n1_ref, n2_ref, n3_ref)
    grp = []
    for gi, (w, dil) in enumerate(A_PATTERN):
        q = qa[:, gi * A_W:(gi + 1) * A_W]
        knew = qa[:, 768 + gi * A_W:768 + (gi + 1) * A_W]
        vnew = qa[:, 1536 + gi * A_W:1536 + (gi + 1) * A_W]
        old = caches[gi][...].reshape(2 * A_W, w)
        ec, en, m, den = _dec_scores(q, A_HEADS, A_HEADS, old[0:A_W], knew, dil, None)
        grp.append((ec, en, m + jnp.log(den), den, old[A_W:2 * A_W], vnew))
        _shift_cache(news[gi], old, jnp.concatenate([knew, vnew], axis=1))
    lmax = functools.reduce(jnp.maximum, [g[2] for g in grp])
    gws = [jnp.exp(g[2] - lmax) for g in grp]
    gtot = functools.reduce(jnp.add, gws)
    o_all = None
    for gi, (ec, en, _, den, vt, vnew) in enumerate(grp):
        wcol = (gws[gi] / gtot) / den
        t = _dot_nt(ec * wcol, vt) + _dot(en * wcol, vnew)
        o_all = t if o_all is None else o_all + t
    oa_ref[...] = _pick_heads(o_all, A_HEADS, A_HEADS).astype(oa_ref.dtype)
    nqc = C_HEADS * HD
    kvw = C_KV * HD
    old = cc_ref[...].reshape(2 * kvw, BLK)
    knew = qc[:, nqc:nqc + kvw]
    vnew = qc[:, nqc + kvw:nqc + 2 * kvw]
    ec, en, m, den = _dec_scores(qc[:, 0:nqc], C_HEADS, C_KV, old[0:kvw], knew, 1, sink_ref[...])
    o_c = _dot_nt(ec / den, old[kvw:2 * kvw]) + _dot(en / den, vnew)
    oc_ref[...] = _pick_heads(o_c, C_HEADS, C_KV).astype(oc_ref.dtype)
    _shift_cache(nc_ref, old, qc[:, nqc:nqc + 2 * kvw])


def _decode_attn(layer, sink_col, qkv_a, qkv_c, caches, prev_out):
    row0 = NP // S

    def cache_spec(c):
        return pl.BlockSpec((None, None) + c.shape[2:], lambda b: (layer, b, 0, 0, 0, 0))

    in_specs = [pl.BlockSpec((C_HEADS * S, 1), lambda b: (0, 0)),
                pl.BlockSpec((S, QKV_A), lambda b: (row0 + b, 0)),
                pl.BlockSpec((S, QKV_C), lambda b: (row0 + b, 0))] + [cache_spec(c) for c in caches]
    args = [sink_col, qkv_a, qkv_c] + list(caches)
    aliases = {}
    if prev_out is not None:
        for k, p in enumerate(prev_out):
            in_specs.append(pl.BlockSpec(memory_space=pl.ANY))
            args.append(p)
            aliases[len(args) - 1] = k
    out_specs = [cache_spec(c) for c in caches] + [
        pl.BlockSpec((S, A_W), lambda b: (b, 0)),
        pl.BlockSpec((S, C_HEADS * HD), lambda b: (b, 0))]
    out_shape = [jax.ShapeDtypeStruct(c.shape, F32) for c in caches] + [
        jax.ShapeDtypeStruct((NS, A_W), BF16), jax.ShapeDtypeStruct((NS, C_HEADS * HD), BF16)]
    return pl.pallas_call(
        _decode_attn_kernel,
        grid=(NSEQ,),
        in_specs=in_specs,
        out_specs=out_specs,
        out_shape=out_shape,
        input_output_aliases=aliases,
        compiler_params=_cp(("arbitrary",), 56),
        name=f"decode_attn{layer}",
    )(*args)


def _ssd_kernel(xbcdt_ref, z_ref, conv0_ref, h0_ref, convw_ref, convb_ref, dtb_ref, arow_ref,
                dskip_ref, nrm_ref, *rest, q_len):
    y_ref, hout_ref, xp_ref, h_ref, yacc_ref = rest[-5:]
    c = pl.program_id(1)
    Q = q_len

    @pl.when(c == 0)
    def _():
        xp_ref[0:8, :] = conv0_ref[...]
        h_ref[...] = h0_ref[...]

    xbc = xbcdt_ref[:, 0:CONV_DIM]
    xp_ref[8:8 + Q, :] = xbc
    w = convw_ref[...]
    acc = convb_ref[...] + w[3:4, :] * xbc
    acc = acc + w[2:3, :] * xp_ref[7:7 + Q, :]
    acc = acc + w[1:2, :] * xp_ref[6:6 + Q, :]
    acc = acc + w[0:1, :] * xp_ref[5:5 + Q, :]
    u = _silu(acc)
    xp_ref[0:8, :] = xbc[Q - 8:Q, :]

    dtp = _softplus(xbcdt_ref[:, CONV_DIM:XBCDT_W] + dtb_ref[...])
    da = dtp * arow_ref[...]
    ri = lax.broadcasted_iota(I32, (Q, Q), 0)
    ci = lax.broadcasted_iota(I32, (Q, Q), 1)
    tri = ri >= ci
    acs = jnp.dot(tri.astype(F32), da, precision=lax.Precision.HIGHEST, preferred_element_type=F32)
    acs_t = acs.T
    dt_t = dtp.T
    hpg = SSM_HEADS // 2
    for g in range(2):
        bg = u[:, SSM_INNER + g * SSM_N:SSM_INNER + (g + 1) * SSM_N]
        cg = u[:, SSM_INNER + (2 + g) * SSM_N:SSM_INNER + (3 + g) * SSM_N]
        cb = _dot_nt(cg, bg)
        for hh in range(hpg):
            hd = g * hpg + hh
            a_col = acs[:, hd:hd + 1]
            seg = a_col - acs_t[hd:hd + 1, :]
            decay = jnp.exp(jnp.where(tri, seg, -jnp.inf))
            mh = cb * decay * dt_t[hd:hd + 1, :]
            xh = u[:, hd * HD:(hd + 1) * HD]
            hst = h_ref[hd]
            y = _dot(mh, xh) + _dot_nt(cg, hst) * jnp.exp(a_col)
            a_last = acs[Q - 1:Q, hd:hd + 1]
            tail = jnp.exp(a_last - a_col) * dtp[:, hd:hd + 1]
            h_ref[hd] = hst * jnp.exp(a_last) + _dot_tn(xh * tail, bg)
            yacc_ref[:, hd * HD:(hd + 1) * HD] = y + dskip_ref[:, hd * HD:(hd + 1) * HD] * xh
    yz = yacc_ref[...] * _silu(z_ref[...])
    half = SSM_INNER // 2
    for g in range(2):
        yg = yz[:, g * half:(g + 1) * half]
        y_ref[:, g * half:(g + 1) * half] = (_rms(yg) * nrm_ref[:, g * half:(g + 1) * half]).astype(y_ref.dtype)

    @pl.when(c == pl.num_programs(1) - 1)
    def _():
        hout_ref[...] = h_ref[...]


def _ssd(layer, xbcdt, z, conv0, h0, convw, convb, dtb, arow, dskip, nrm, nseq, q_len, row0, h_prev=None):
    nchunks = (NP if nseq == 1 else S) // q_len
    rows = nseq * nchunks * q_len
    vec = lambda n: pl.BlockSpec((1, n), lambda b, c: (0, 0))
    state = pl.BlockSpec((None, None, SSM_HEADS, HD, SSM_N), lambda b, c: (layer, b, 0, 0, 0))
    in_specs = [pl.BlockSpec((q_len, XBCDT_W), lambda b, c: (row0 + b * nchunks + c, 0)),
                pl.BlockSpec((q_len, SSM_INNER), lambda b, c: (row0 + b * nchunks + c, 0)),
                pl.BlockSpec((None, 8, CONV_DIM), lambda b, c: (b, 0, 0)),
                state,
                pl.BlockSpec((4, CONV_DIM), lambda b, c: (0, 0)),
                vec(CONV_DIM), vec(128), vec(128), vec(SSM_INNER), vec(SSM_INNER)]
    args = [xbcdt, z, conv0, h0, convw, convb, dtb, arow, dskip, nrm]
    aliases = {}
    if h_prev is not None:
        in_specs.append(pl.BlockSpec(memory_space=pl.ANY))
        args.append(h_prev)
        aliases[len(args) - 1] = 1
    return pl.pallas_call(
        functools.partial(_ssd_kernel, q_len=q_len),
        grid=(nseq, nchunks),
        in_specs=in_specs,
        out_specs=[pl.BlockSpec((q_len, SSM_INNER), lambda b, c: (b * nchunks + c, 0)), state],
        out_shape=[jax.ShapeDtypeStruct((rows, SSM_INNER), BF16), jax.ShapeDtypeStruct(h0.shape, F32)],
        scratch_shapes=[pltpu.VMEM((q_len + 8, CONV_DIM), F32),
                        pltpu.VMEM((SSM_HEADS, HD, SSM_N), F32),
                        pltpu.VMEM((q_len, SSM_INNER), F32)],
        input_output_aliases=aliases,
        compiler_params=_cp(("arbitrary", "arbitrary")),
        name=f"ssd_q{q_len}",
    )(*args)


def _merge_kernel(h_ref, oa_ref, ob_ref, oc_ref, g0_ref, g1_ref, g2_ref, wa_ref, wb_ref, wc_ref, o_ref):
    h = h_ref[...]
    acc = _sigmoid(_dot(h, g0_ref[...])) * _dot(oa_ref[...], wa_ref[...])
    acc = acc + _sigmoid(_dot(h, g1_ref[...])) * _dot(ob_ref[...], wb_ref[...])
    acc = acc + _sigmoid(_dot(h, g2_ref[...])) * _dot(oc_ref[...], wc_ref[...])
    o_ref[...] = acc.astype(o_ref.dtype)


def _merge(layer, h1, oa, ob, oc, wgate, w_br_a, w_br_b, w_br_c):
    tm, tn = 1024, 512
    nj = D // tn
    row = lambda k: pl.BlockSpec((tm, k), lambda i, j: (i, 0))
    gate = lambda b: pl.BlockSpec((D, tn), lambda i, j: (0, b * nj + j))
    br = lambda k: pl.BlockSpec((None, k, tn), lambda i, j: (layer, 0, j))
    return pl.pallas_call(
        _merge_kernel,
        grid=(T // tm, nj),
        in_specs=[row(D), row(A_W), row(SSM_INNER), row(C_HEADS * HD), gate(0), gate(1), gate(2),
                  br(A_W), br(SSM_INNER), br(C_HEADS * HD)],
        out_specs=pl.BlockSpec((tm, tn), lambda i, j: (i, j)),
        out_shape=jax.ShapeDtypeStruct((T, D), BF16),
        compiler_params=_cp(("parallel", "arbitrary"), 56),
        name="merge",
    )(h1, oa, ob, oc, wgate, wgate, wgate, w_br_a, w_br_b, w_br_c)


def _outproj_kernel(m_ref, w_ref, x_ref, gpost_ref, gpre_ref, gap_ref, gas_ref, shp_ref, shs_ref,
                    scp_ref, scs_ref, x_out_ref, hf_ref, hp_ref):
    is_p = pl.program_id(0) < NPT_TOK
    ga = _sel_tile(is_p, gap_ref, gas_ref)
    sh = _sel_tile(is_p, shp_ref, shs_ref)
    sc = _sel_tile(is_p, scp_ref, scs_ref)
    mix = _dot(m_ref[...], w_ref[...])
    xn = x_ref[...] + ga * (_rms(mix) * gpost_ref[...])
    x_out_ref[...] = xn
    hf = (_rms(xn) * gpre_ref[...]) * (1.0 + sc) + sh
    hf_ref[...] = hf
    _store_token_tiles(hp_ref, _pack_bf16_pairs(hf))


def _outproj(merged, w_out_bf, x, g_post, g_pre, mod_p, mod_s):
    tok = pl.BlockSpec((TM_TOK, D), lambda i: (i, 0))
    vec = pl.BlockSpec((1, D), lambda i: (0, 0))
    return pl.pallas_call(
        _outproj_kernel,
        grid=(T // TM_TOK,),
        in_specs=[tok, pl.BlockSpec((D, D), lambda i: (0, 0)), tok, vec, vec]
        + _mod_specs(2) + _mod_specs(3) + _mod_specs(4),
        out_specs=[tok, tok, pl.BlockSpec((TM_TOK * XS_R, LANES), lambda i: (i, 0))],
        out_shape=[jax.ShapeDtypeStruct((T, D), F32)] * 2
        + [jax.ShapeDtypeStruct((T * XS_R, LANES), jnp.uint32)],
        compiler_params=_cp(("parallel",), 56),
        name="outproj",
    )(merged, w_out_bf, x, g_post.reshape(1, D), g_pre.reshape(1, D), mod_p, mod_s, mod_p, mod_s, mod_p, mod_s)


def _router_kernel(hf_ref, rwt_ref, bias_ref, topi_ref, topw_ref):
    tm = hf_ref.shape[0]
    logits = lax.dot_general(rwt_ref[...], hf_ref[...], (((1,), (1,)), ((), ())),
                             precision=lax.Precision.HIGHEST, preferred_element_type=F32)
    scores = _sigmoid(logits)
    sel = scores + bias_ref[...]
    ng = 8
    sel3 = sel.reshape(ng, N_EXP // ng, tm)
    idx3 = lax.broadcasted_iota(I32, sel3.shape, 1)
    m1 = jnp.max(sel3, axis=1, keepdims=True)
    first = jnp.min(jnp.where(sel3 == m1, idx3, N_EXP), axis=1, keepdims=True)
    m2 = jnp.max(jnp.where(idx3 == first, -jnp.inf, sel3), axis=1, keepdims=True)
    gs = m1 + m2
    gidx = lax.broadcasted_iota(I32, (ng, 1, tm), 0)
    gmask = jnp.zeros((ng, 1, tm), jnp.bool_)
    for _ in range(4):
        m = jnp.max(gs, axis=0, keepdims=True)
        f = jnp.min(jnp.where(gs == m, gidx, ng), axis=0, keepdims=True)
        hit = gidx == f
        gmask = jnp.logical_or(gmask, hit)
        gs = jnp.where(hit, -jnp.inf, gs)
    masked = jnp.where(gmask, sel3, -jnp.inf).reshape(N_EXP, tm)
    eidx = lax.broadcasted_iota(I32, (N_EXP, tm), 0)
    tis, tws = [], []
    for _ in range(TOP_K):
        m = jnp.max(masked, axis=0, keepdims=True)
        f = jnp.min(jnp.where(masked == m, eidx, N_EXP), axis=0, keepdims=True)
        hit = eidx == f
        tis.append(f)
        tws.append(jnp.sum(jnp.where(hit, scores, 0.0), axis=0, keepdims=True))
        masked = jnp.where(hit, -jnp.inf, masked)
    wsum = functools.reduce(jnp.add, tws)
    topi_ref[...] = jnp.concatenate(tis, axis=0)
    topw_ref[...] = jnp.concatenate([w / wsum * ROUTED_SCALE for w in tws], axis=0)


def _router(hf, rwt, bias_col):
    tm = 512
    return pl.pallas_call(
        _router_kernel,
        grid=(T // tm,),
        in_specs=[pl.BlockSpec((tm, D), lambda i: (i, 0)),
                  pl.BlockSpec((N_EXP, D), lambda i: (0, 0)),
                  pl.BlockSpec((N_EXP, 1), lambda i: (0, 0))],
        out_specs=[pl.BlockSpec((TOP_K, tm), lambda i: (0, i))] * 2,
        out_shape=[jax.ShapeDtypeStruct((TOP_K, T), I32), jax.ShapeDtypeStruct((TOP_K, T), F32)],
        compiler_params=_cp(("parallel",)),
        name="router",
    )(hf, rwt, bias_col)


def _route_tables(topi):
    onehot = topi[None] == jnp.arange(N_EXP, dtype=I32)[:, None, None]
    tok = jnp.any(onehot, axis=1).astype(I32)
    incl = jnp.cumsum(tok, axis=1)
    counts = incl[:, -1]
    padded = (counts + BM - 1) // BM * BM
    pad_end = jnp.cumsum(padded)
    base = (pad_end - padded)[:, None] + incl - tok
    dest = jnp.sum(jnp.where(onehot, base[:, None, :], 0), axis=0).astype(I32)
    starts = jnp.arange(NB, dtype=I32) * BM
    block_e = jnp.minimum(jnp.sum(pad_end[None, :] <= starts[:, None], axis=1), N_EXP - 1).astype(I32)
    n_used = (pad_end[-1] // BM).astype(I32).reshape(1)
    prev_e = jnp.concatenate([jnp.full((1,), -1, I32), block_e[:-1]])
    first = (block_e != prev_e).astype(I32)
    eid = jnp.arange(N_EXP, dtype=I32)
    later = jnp.where((counts[None, :] > 0) & (eid[None, :] > eid[:, None]), eid[None, :], N_EXP)
    nxt = jnp.min(later, axis=1)
    nxt = jnp.where(nxt == N_EXP, -1, nxt).astype(I32)
    next_e = jnp.sum(jnp.where(block_e[:, None] == eid[None, :], nxt[None, :], 0), axis=1).astype(I32)
    return dest, block_e, n_used, first, next_e


def _dispatch_kernel(dest_ref, hf_ref, xs_in_ref, xs_ref, sem):
    del xs_in_ref

    def body(t, carry):
        for k in range(TOP_K):
            src = hf_ref.at[pl.ds(pl.multiple_of(t * XS_R, XS_R), XS_R)]
            dst = xs_ref.at[pl.ds(pl.multiple_of(dest_ref[k, t] * XS_R, XS_R), XS_R)]
            pltpu.make_async_copy(src, dst, sem).start()
        return carry

    lax.fori_loop(0, TT_DISP, body, 0)
    for k in range(TOP_K):
        pltpu.make_async_copy(hf_ref, xs_ref.at[pl.ds(0, TT_DISP * XS_R)], sem).wait()


def _dispatch(dest, hf, xs_buf):
    return pl.pallas_call(
        _dispatch_kernel,
        grid=(T // TT_DISP,),
        in_specs=[pl.BlockSpec((TOP_K, TT_DISP), lambda i: (0, i), memory_space=pltpu.SMEM),
                  pl.BlockSpec((TT_DISP * XS_R, LANES), lambda i: (i, 0)),
                  pl.BlockSpec(memory_space=pl.ANY)],
        out_specs=pl.BlockSpec(memory_space=pl.ANY),
        out_shape=jax.ShapeDtypeStruct((P_ROWS * XS_R, LANES), jnp.uint32),
        scratch_shapes=[pltpu.SemaphoreType.DMA(())],
        input_output_aliases={2: 0},
        compiler_params=_cp(("arbitrary",)),
        name="moe_dispatch",
    )(dest, hf, xs_buf)


def _swiglu(x, wg, wu, wd):
    return _dot(_silu(_dot(x, wg)) * _dot(x, wu), wd)


def _expert_kernel(be_ref, nu_ref, first_ref, next_ref, x_ref, wg_hbm, wu_hbm, wd_hbm, o_ref,
                   stage_g, stage_u, stage_d, wg_bf, wu_bf, wd_bf, sems, *, layer):
    i = pl.program_id(0)
    used = i < nu_ref[0]

    def weight_copies(e):
        return [pltpu.make_async_copy(wg_hbm.at[layer, e], stage_g, sems.at[0]),
                pltpu.make_async_copy(wu_hbm.at[layer, e], stage_u, sems.at[1]),
                pltpu.make_async_copy(wd_hbm.at[layer, e], stage_d, sems.at[2])]

    @pl.when(i == 0)
    def _():
        for c in weight_copies(be_ref[0]):
            c.start()

    @pl.when(jnp.logical_and(used, first_ref[i] == 1))
    def _():
        for c in weight_copies(be_ref[i]):
            c.wait()
        wg_bf[...] = stage_g[...].astype(BF16)
        wu_bf[...] = stage_u[...].astype(BF16)
        wd_bf[...] = stage_d[...].astype(BF16)

        @pl.when(next_ref[i] >= 0)
        def _():
            for c in weight_copies(next_ref[i]):
                c.start()

    @pl.when(used)
    def _():
        x = _unpack_bf16_pairs(_load_token_tiles(x_ref, (), BM, XS_R))
        o_ref[...] = _swiglu(x, wg_bf[...], wu_bf[...], wd_bf[...])

    @pl.when(jnp.logical_not(used))
    def _():
        o_ref[...] = jnp.zeros_like(o_ref)


def _experts(layer, xs, block_e, n_used, first, next_e, exp_gate, exp_up, exp_down):
    hbm = pl.BlockSpec(memory_space=pl.ANY)
    return pl.pallas_call(
        functools.partial(_expert_kernel, layer=layer),
        grid_spec=pltpu.PrefetchScalarGridSpec(
            num_scalar_prefetch=4,
            grid=(NB,),
            in_specs=[pl.BlockSpec((BM * XS_R, LANES), lambda i, *_: (i, 0)), hbm, hbm, hbm],
            out_specs=pl.BlockSpec((BM, D), lambda i, *_: (i, 0)),
            scratch_shapes=[pltpu.VMEM((D, MOE_H), F32), pltpu.VMEM((D, MOE_H), F32),
                            pltpu.VMEM((MOE_H, D), F32),
                            pltpu.VMEM((D, MOE_H), BF16), pltpu.VMEM((D, MOE_H), BF16),
                            pltpu.VMEM((MOE_H, D), BF16),
                            pltpu.SemaphoreType.DMA((3,))]),
        out_shape=jax.ShapeDtypeStruct((P_ROWS, D), F32),
        compiler_params=_cp(("arbitrary",), 48),
        name="moe_experts",
    )(block_e, n_used, first, next_e, xs, exp_gate, exp_up, exp_down)


def _shared_kernel(x_ref, wg_ref, wu_ref, wd_ref, o_ref):
    o_ref[...] = _swiglu(x_ref[...], wg_ref[...], wu_ref[...], wd_ref[...])


def _shared_expert(layer, hf, wg, wu, wd):
    tm = 512
    wspec = lambda a, b: pl.BlockSpec((None, a, b), lambda i: (layer, 0, 0))
    return pl.pallas_call(
        _shared_kernel,
        grid=(T // tm,),
        in_specs=[pl.BlockSpec((tm, D), lambda i: (i, 0)), wspec(D, MOE_H), wspec(D, MOE_H), wspec(MOE_H, D)],
        out_specs=pl.BlockSpec((tm, D), lambda i: (i, 0)),
        out_shape=jax.ShapeDtypeStruct((T, D), F32),
        compiler_params=_cp(("parallel",), 56),
        name="moe_shared",
    )(hf, wg, wu, wd)


def _moe_combine_kernel(dest_ref, dnext_ref, ys_ref, w_ref, fsh_ref, x_ref, gpost_ref, gap_ref, gas_ref,
                        o_ref, gbuf, sems):
    tt = TT_DISP
    i = pl.program_id(0)
    is_p = i < NP // tt
    slot = i % 2

    def gather(d_ref, s):
        def body(t, carry):
            for k in range(TOP_K):
                pltpu.make_async_copy(ys_ref.at[pl.ds(d_ref[k, t], 1)], gbuf.at[s, k, pl.ds(t, 1)],
                                      sems.at[s]).start()
            return carry
        lax.fori_loop(0, tt, body, 0)

    @pl.when(i == 0)
    def _():
        gather(dest_ref, 0)

    @pl.when(i + 1 < pl.num_programs(0))
    def _():
        gather(dnext_ref, 1 - slot)

    for k in range(TOP_K):
        pltpu.make_async_copy(ys_ref.at[pl.ds(0, tt)], gbuf.at[slot, k], sems.at[slot]).wait()
    w = w_ref[...]
    f = fsh_ref[...]
    for k in range(TOP_K):
        f = f + gbuf[slot, k] * w[:, k:k + 1]
    ga = jnp.where(is_p, gap_ref[0:1, :], gas_ref[...])
    o_ref[...] = x_ref[...] + ga * (_rms(f) * gpost_ref[...])


def _moe_combine(dest, ys, topw_t, f_shared, x, g_post, mod_p, mod_s):
    tt = TT_DISP
    tok = pl.BlockSpec((tt, D), lambda i: (i, 0))
    return pl.pallas_call(
        _moe_combine_kernel,
        grid=(T // tt,),
        in_specs=[pl.BlockSpec((TOP_K, tt), lambda i: (0, i), memory_space=pltpu.SMEM),
                  pl.BlockSpec((TOP_K, tt), lambda i: (0, jnp.minimum(i + 1, T // tt - 1)),
                               memory_space=pltpu.SMEM),
                  pl.BlockSpec(memory_space=pl.ANY),
                  pl.BlockSpec((tt, TOP_K), lambda i: (i, 0)),
                  tok, tok, pl.BlockSpec((1, D), lambda i: (0, 0)),
                  pl.BlockSpec((8, D), lambda i: (0, 5)),
                  pl.BlockSpec((tt, D), lambda i: (jnp.maximum(i - NP // tt, 0), 5))],
        out_specs=tok,
        out_shape=jax.ShapeDtypeStruct((T, D), F32),
        scratch_shapes=[pltpu.VMEM((2, TOP_K, tt, D), F32), pltpu.SemaphoreType.DMA((2,))],
        compiler_params=_cp(("arbitrary",)),
        name="moe_combine",
    )(dest, dest, ys, topw_t, f_shared, x, g_post.reshape(1, D), mod_p, mod_s)


def _pad_cols(w, n):
    return jnp.pad(w, ((0, 0), (0, n - w.shape[1])))


def _layer(l, x, mod_p, mod_s, p, dec_caches, dec_prev, ssm_prev, xs_buf):
    w_in = p['w_in'][l]
    h1 = _prenorm(x, p['norm_pre_mix'][l], mod_p, mod_s)
    qkv_a = _matmul(h1, w_in[:, OFF_QKV_A:OFF_QKV_C].astype(BF16), 1024, 768, "proj_qkv_a")
    qkv_c = _matmul(h1, w_in[:, OFF_QKV_C:OFF_Z].astype(BF16), 1024, QKV_C, "proj_qkv_c")
    z = _matmul(h1, w_in[:, OFF_Z:OFF_XBC].astype(BF16), 1024, SSM_INNER, "proj_z")
    xbcdt = _matmul(h1, _pad_cols(w_in[:, OFF_XBC:OFF_GATES], XBCDT_W).astype(BF16), 1024, XBCDT_W, "proj_xbcdt")

    outs, lses = zip(*[_banded_a(qkv_a, gi, dil) for gi, (_, dil) in enumerate(A_PATTERN)])
    oa_p = _combine_groups(outs, lses)
    oc_p = _banded_c(qkv_c, p['sinks'][l])
    sink_col = jnp.repeat(p['sinks'][l], S).reshape(C_HEADS * S, 1)
    n1, n2, n3, ncc, oa_s, oc_s = _decode_attn(l, sink_col, qkv_a, qkv_c, dec_caches, dec_prev)

    dtb = jnp.pad(p['dt_bias'][l], (0, 128 - SSM_HEADS)).reshape(1, 128)
    arow = jnp.pad(-jnp.exp(p['a_log'][l]), (0, 128 - SSM_HEADS)).reshape(1, 128)
    dskip = jnp.repeat(p['d_skip'][l], HD).reshape(1, SSM_INNER)
    nrm = p['ssm_norm'][l].reshape(1, SSM_INNER)
    convw = p['conv_w'][l]
    convb = p['conv_b'][l].reshape(1, CONV_DIM)
    ob_p, ssm_p = _ssd(0, xbcdt, z, jnp.zeros((1, 8, CONV_DIM), F32),
                       jnp.zeros((1, 1, SSM_HEADS, HD, SSM_N), F32),
                       convw, convb, dtb, arow, dskip, nrm, 1, BLK, 0)
    conv0_s = jnp.pad(p['state_conv'][l], ((0, 0), (5, 0), (0, 0)))
    ob_s, ssm_s = _ssd(l, xbcdt, z, conv0_s, p['state_ssm'],
                       convw, convb, dtb, arow, dskip, nrm, NSEQ, S, NP // S, h_prev=ssm_prev)

    oa = jnp.concatenate([oa_p, oa_s], axis=0)
    ob = jnp.concatenate([ob_p, ob_s], axis=0)
    oc = jnp.concatenate([oc_p, oc_s], axis=0)
    merged = _merge(l, h1, oa, ob, oc, w_in[:, OFF_GATES:].astype(BF16), p['w_br_a'], p['w_br_b'], p['w_br_c'])
    x, hf, hf_packed = _outproj(merged, p['w_out'][l].astype(BF16), x, p['norm_post_mix'][l],
                                p['norm_pre_ffn'][l], mod_p, mod_s)

    topi, topw = _router(hf, p['router_w'][l].T, p['router_bias'][l].reshape(N_EXP, 1))
    dest, block_e, n_used, first, next_e = _route_tables(topi)
    xs = _dispatch(dest, hf_packed, xs_buf)
    ys = _experts(l, xs, block_e, n_used, first, next_e, p['exp_gate'], p['exp_up'], p['exp_down'])
    f_shared = _shared_expert(l, hf, p['shared_gate'], p['shared_up'], p['shared_down'])
    x = _moe_combine(dest, ys, topw.T, f_shared, x, p['norm_post_ffn'][l], mod_p, mod_s)

    def last_kv(src, k_off, v_off, rows, heads):
        k = src[NP - rows:NP, k_off:k_off + heads * HD].reshape(rows, heads, HD)
        v = src[NP - rows:NP, v_off:v_off + heads * HD].reshape(rows, heads, HD)
        return jnp.stack([k, v], axis=1)[None]

    a_states = [last_kv(qkv_a, 768 + gi * A_W, 1536 + gi * A_W, w, A_HEADS)
                for gi, (w, _) in enumerate(A_PATTERN)]
    c_state = last_kv(qkv_c, C_HEADS * HD, C_HEADS * HD + C_KV * HD, BLK, C_KV)
    conv_p = xbcdt[NP - 3:NP, :CONV_DIM][None]
    conv_s = xbcdt[NP:, :CONV_DIM].reshape(NSEQ, S, CONV_DIM)[:, S - 3:]
    prompt_state = (*a_states, c_state, ssm_p.reshape(1, 2, SSM_HEADS // 2, HD, SSM_N), conv_p)
    return x, prompt_state, (n1, n2, n3, ncc), ssm_s, conv_s, xs


def kernel(x_prompt, x_sample, cache_a1_kv, cache_a2_kv, cache_a3_kv, cache_c_kv, state_ssm, state_conv,
           c_prompt, c_sample, ada_w, ada_b, norm_pre_mix, norm_post_mix, norm_pre_ffn, norm_post_ffn,
           w_in, conv_w, conv_b, dt_bias, a_log, d_skip, ssm_norm, sinks, w_br_a, w_br_b, w_br_c, w_out,
           router_w, router_bias, exp_gate, exp_up, exp_down, shared_gate, shared_up, shared_down):
    p = dict(norm_pre_mix=norm_pre_mix, norm_post_mix=norm_post_mix, norm_pre_ffn=norm_pre_ffn,
             norm_post_ffn=norm_post_ffn, w_in=w_in, conv_w=conv_w, conv_b=conv_b, dt_bias=dt_bias,
             a_log=a_log, d_skip=d_skip, ssm_norm=ssm_norm, sinks=sinks, w_br_a=w_br_a, w_br_b=w_br_b,
             w_br_c=w_br_c, w_out=w_out, router_w=router_w, router_bias=router_bias, exp_gate=exp_gate,
             exp_up=exp_up, exp_down=exp_down, shared_gate=shared_gate, shared_up=shared_up,
             shared_down=shared_down, state_conv=state_conv,
             state_ssm=state_ssm.reshape(DEPTH, NSEQ, SSM_HEADS, HD, SSM_N))
    x = jnp.concatenate([x_prompt.reshape(NP, D), x_sample.reshape(NS, D)], axis=0)
    c_all = jnp.concatenate([c_prompt, c_sample, jnp.zeros((7, D), F32)], axis=0)
    mod = _ada(c_all, ada_w, ada_b)
    dec_caches = tuple(jnp.transpose(c, (0, 1, 3, 4, 5, 2))
                       for c in (cache_a1_kv, cache_a2_kv, cache_a3_kv, cache_c_kv))
    xs_buf = jnp.zeros((P_ROWS * XS_R, LANES), jnp.uint32)
    dec_prev = ssm_s = None
    prompt_states, conv_s = [], []
    for l in range(DEPTH):
        mod_s = jnp.repeat(mod[l, 1:1 + NSEQ], S, axis=0)
        x, pst, dec_prev, ssm_s, conv_l, xs_buf = _layer(l, x, mod[l], mod_s, p, dec_caches, dec_prev,
                                                          ssm_s, xs_buf)
        prompt_states.append(pst)
        conv_s.append(conv_l)
    a1_p, a2_p, a3_p, c_p, ssm_p, conv_p = (jnp.stack(t) for t in zip(*prompt_states))
    n1, n2, n3, ncc = (jnp.transpose(c, (0, 1, 5, 2, 3, 4)) for c in dec_prev)
    y_p = x[:NP].reshape(1, NP, D)
    y_s = x[NP:].reshape(NSEQ, S, D)
    return (y_p, y_s, a1_p, n1, a2_p, n2, a3_p, n3, c_p, ncc,
            ssm_p, ssm_s.reshape(state_ssm.shape), conv_p, jnp.stack(conv_s))
```

```python
import functools
import math

import jax
import jax.numpy as jnp
from jax import lax
from jax.experimental import pallas as pl
from jax.experimental.pallas import tpu as pltpu

F32 = jnp.float32
BF16 = jnp.bfloat16
I32 = jnp.int32

D = 2048
NP = 8192
NSEQ = 128
S = 8
NS = NSEQ * S
T = NP + NS
DEPTH = 2
HD = 64
EPS = 1e-6
A_PATTERN = ((128, 1), (512, 4), (2048, 16))
A_HEADS = 4
A_W = 256
QKV_A = 2304
C_HEADS = 12
C_KV = 3
QKV_C = 1152
SSM_INNER = 1024
SSM_HEADS = 16
SSM_N = 128
CONV_DIM = 1536
XBCDT_W = 1664
N_EXP = 64
TOP_K = 8
MOE_H = 512
ROUTED_SCALE = 2.5
BLK = 128

OFF_QKV_A = 0
OFF_QKV_C = 2304
OFF_Z = 3456
OFF_XBC = 4480
OFF_DT = 6016
OFF_GATES = 6032

TM_TOK = 256
NPT_TOK = NP // TM_TOK
BM = 256
NB = T * TOP_K // BM + N_EXP
P_ROWS = NB * BM
TT_DISP = 128
LANES = 128
XS_R = D // 2 // LANES


def _cp(sem, vmem_mb=48):
    return pltpu.CompilerParams(dimension_semantics=sem, vmem_limit_bytes=vmem_mb << 20)


def _sigmoid(x):
    return 1.0 / (1.0 + jnp.exp(-x))


def _silu(x):
    return x * _sigmoid(x)


def _softplus(x):
    return jnp.maximum(x, 0.0) + jnp.log1p(jnp.exp(-jnp.abs(x)))


def _rms(x):
    return x * lax.rsqrt(jnp.mean(x * x, axis=-1, keepdims=True) + EPS)


def _dot(a, b):
    return jnp.dot(a.astype(BF16), b.astype(BF16), preferred_element_type=F32)


def _dot_nt(a, b):
    return lax.dot_general(a.astype(BF16), b.astype(BF16), (((1,), (1,)), ((), ())),
                           preferred_element_type=F32)


def _dot_tn(a, b):
    return lax.dot_general(a.astype(BF16), b.astype(BF16), (((0,), (0,)), ((), ())),
                           preferred_element_type=F32)


def _pack_bf16_pairs(x):
    half = x.shape[1] // 2
    bits = lax.bitcast_convert_type(x.astype(BF16).astype(F32), jnp.uint32)
    return (bits[:, :half] >> 16) | (bits[:, half:] & jnp.uint32(0xFFFF0000))


def _unpack_bf16_pairs(w):
    lo = lax.bitcast_convert_type(w << 16, F32)
    hi = lax.bitcast_convert_type(w & jnp.uint32(0xFFFF0000), F32)
    return jnp.concatenate([lo.astype(BF16), hi.astype(BF16)], axis=1)


def _store_token_tiles(ref, val):
    m, c = val.shape[0], val.shape[1] // LANES
    for j in range(c):
        ref[pl.ds(j, m, stride=c), :] = val[:, j * LANES:(j + 1) * LANES]


def _load_token_tiles(ref, lead, m, c):
    return jnp.concatenate([ref[(*lead, pl.ds(j, m, stride=c), slice(None))] for j in range(c)], axis=1)


def _sel_tile(is_prompt, p_ref, s_ref):
    return jnp.where(is_prompt, p_ref[0:1, :], s_ref[...])


def _ada_kernel(c_ref, w_ref, b_ref, o_ref):
    o_ref[...] = _dot(_silu(c_ref[...]), w_ref[...]) + b_ref[...]


def _ada(c_all, ada_w, ada_b):
    rows = c_all.shape[0]
    tn = 1024
    return pl.pallas_call(
        _ada_kernel,
        grid=(DEPTH, 6 * D // tn),
        in_specs=[pl.BlockSpec((rows, D), lambda l, j: (0, 0)),
                  pl.BlockSpec((None, D, tn), lambda l, j: (l, 0, j)),
                  pl.BlockSpec((None, 1, tn), lambda l, j: (l, 0, j))],
        out_specs=pl.BlockSpec((None, rows, tn), lambda l, j: (l, 0, j)),
        out_shape=jax.ShapeDtypeStruct((DEPTH, rows, 6 * D), F32),
        compiler_params=_cp(("arbitrary", "arbitrary")),
        name="ada",
    )(c_all, ada_w, ada_b.reshape(DEPTH, 1, 6 * D))


def _mod_specs(col):
    return [pl.BlockSpec((8, D), lambda i: (0, col)),
            pl.BlockSpec((TM_TOK, D), lambda i: (jnp.maximum(i - NPT_TOK, 0), col))]


def _prenorm_kernel(x_ref, g_ref, shp_ref, shs_ref, scp_ref, scs_ref, o_ref):
    is_p = pl.program_id(0) < NPT_TOK
    sh = _sel_tile(is_p, shp_ref, shs_ref)
    sc = _sel_tile(is_p, scp_ref, scs_ref)
    y = _rms(x_ref[...]) * g_ref[...]
    o_ref[...] = (y * (1.0 + sc) + sh).astype(o_ref.dtype)


def _prenorm(x, g, mod_p, mod_s):
    return pl.pallas_call(
        _prenorm_kernel,
        grid=(T // TM_TOK,),
        in_specs=[pl.BlockSpec((TM_TOK, D), lambda i: (i, 0)),
                  pl.BlockSpec((1, D), lambda i: (0, 0))] + _mod_specs(0) + _mod_specs(1),
        out_specs=pl.BlockSpec((TM_TOK, D), lambda i: (i, 0)),
        out_shape=jax.ShapeDtypeStruct((T, D), BF16),
        compiler_params=_cp(("parallel",)),
        name="prenorm",
    )(x, g.reshape(1, D), mod_p, mod_s, mod_p, mod_s)


def _mm_kernel(x_ref, w_ref, o_ref):
    o_ref[...] = _dot(x_ref[...], w_ref[...]).astype(o_ref.dtype)


def _matmul(x, w, tm, tn, name):
    m, k = x.shape
    n = w.shape[1]
    return pl.pallas_call(
        _mm_kernel,
        grid=(m // tm, n // tn),
        in_specs=[pl.BlockSpec((tm, k), lambda i, j: (i, 0)),
                  pl.BlockSpec((k, tn), lambda i, j: (0, j))],
        out_specs=pl.BlockSpec((tm, tn), lambda i, j: (i, j)),
        out_shape=jax.ShapeDtypeStruct((m, n), F32),
        compiler_params=_cp(("parallel", "arbitrary")),
        name=name,
    )(x, w)


def _banded_kernel(*refs, nh, k_off, v_off, with_sink, with_lse):
    refs = list(refs)
    sink_ref = refs.pop(0) if with_sink else None
    q_ref, kc_ref, kp_ref, vc_ref, vp_ref = refs[:5]
    o_ref = refs[5]
    lse_ref = refs[6] if with_lse else None
    has_prev = pl.program_id(1) > 0
    row = lax.broadcasted_iota(I32, (BLK, 2 * BLK), 0)
    col = lax.broadcasted_iota(I32, (BLK, 2 * BLK), 1)
    dist = row + BLK - col
    valid = (dist >= 0) & (dist <= BLK) & (has_prev | (col >= BLK))
    q = q_ref[...]
    kc, kp, vc, vp = kc_ref[...], kp_ref[...], vc_ref[...], vp_ref[...]
    for h in range(nh):
        qh = q[:, h * HD:(h + 1) * HD]
        ko, vo = k_off[h], v_off[h]
        kk = jnp.concatenate([kp[:, ko:ko + HD], kc[:, ko:ko + HD]], axis=0)
        vv = jnp.concatenate([vp[:, vo:vo + HD], vc[:, vo:vo + HD]], axis=0)
        s = _dot_nt(qh, kk) * (1.0 / math.sqrt(HD))
        s = jnp.where(valid, s, -jnp.inf)
        m = jnp.max(s, axis=-1, keepdims=True)
        if with_sink:
            sk = sink_ref[h]
            m = jnp.maximum(m, sk)
        e = jnp.exp(s - m)
        den = jnp.sum(e, axis=-1, keepdims=True)
        if with_sink:
            den = den + jnp.exp(sk - m)
        o = _dot(e / den, vv)
        o_ref[:, h * HD:(h + 1) * HD] = o.astype(o_ref.dtype)
        if with_lse:
            lse_ref[:, h * HD:(h + 1) * HD] = jnp.broadcast_to(m + jnp.log(den), (BLK, HD))


def _banded_a(qkv, gi, dil):
    cols = jnp.concatenate([qkv[:NP, o + gi * A_W:o + (gi + 1) * A_W] for o in (0, 768, 1536)], axis=1)
    view = cols.astype(BF16).reshape(NP // dil, dil * 3 * A_W)
    nb = NP // dil // BLK
    cpr = 3

    def spec(off, prev):
        off = off // 3
        if prev:
            return pl.BlockSpec((BLK, A_W), lambda r, j: (jnp.maximum(j - 1, 0), r * cpr + off))
        return pl.BlockSpec((BLK, A_W), lambda r, j: (j, r * cpr + off))

    offs = tuple(h * HD for h in range(A_HEADS))
    out_spec = pl.BlockSpec((BLK, A_W), lambda r, j: (j, r))
    o, lse = pl.pallas_call(
        functools.partial(_banded_kernel, nh=A_HEADS, k_off=offs, v_off=offs,
                          with_sink=False, with_lse=True),
        grid=(dil, nb),
        in_specs=[spec(0, False), spec(3, False), spec(3, True), spec(6, False), spec(6, True)],
        out_specs=[out_spec, out_spec],
        out_shape=[jax.ShapeDtypeStruct((NP // dil, dil * A_W), F32)] * 2,
        compiler_params=_cp(("parallel", "arbitrary")),
        name=f"banded_a{gi}",
    )(view, view, view, view, view)
    return o.reshape(NP, A_W), lse.reshape(NP, A_W)


def _banded_c(qkv, sinks):
    nq = C_HEADS * HD
    kvw = 2 * C_KV * HD
    k_off = tuple((h // 4) * HD for h in range(C_HEADS))
    v_off = tuple(C_KV * HD + (h // 4) * HD for h in range(C_HEADS))
    cur = pl.BlockSpec((BLK, kvw), lambda r, j: (j, nq // kvw))
    prev = pl.BlockSpec((BLK, kvw), lambda r, j: (jnp.maximum(j - 1, 0), nq // kvw))
    return pl.pallas_call(
        functools.partial(_banded_kernel, nh=C_HEADS, k_off=k_off, v_off=v_off,
                          with_sink=True, with_lse=False),
        grid=(1, NP // BLK),
        in_specs=[pl.BlockSpec(memory_space=pltpu.SMEM),
                  pl.BlockSpec((BLK, nq), lambda r, j: (j, 0)), cur, prev, cur, prev],
        out_specs=pl.BlockSpec((BLK, nq), lambda r, j: (j, 0)),
        out_shape=jax.ShapeDtypeStruct((NP, nq), BF16),
        compiler_params=_cp(("parallel", "arbitrary")),
        name="banded_c",
    )(sinks, qkv, qkv, qkv, qkv, qkv)


def _combine_kernel(o1, o2, o3, l1, l2, l3, out_ref):
    a, b, c = l1[...], l2[...], l3[...]
    m = jnp.maximum(jnp.maximum(a, b), c)
    ea, eb, ec = jnp.exp(a - m), jnp.exp(b - m), jnp.exp(c - m)
    tot = ea + eb + ec
    out_ref[...] = ((ea / tot) * o1[...] + (eb / tot) * o2[...] + (ec / tot) * o3[...]).astype(out_ref.dtype)


def _combine_groups(outs, lses):
    tm = 512
    spec = pl.BlockSpec((tm, A_W), lambda i: (i, 0))
    return pl.pallas_call(
        _combine_kernel,
        grid=(NP // tm,),
        in_specs=[spec] * 6,
        out_specs=spec,
        out_shape=jax.ShapeDtypeStruct((NP, A_W), BF16),
        compiler_params=_cp(("parallel",)),
        name="combine_groups",
    )(*outs, *lses)


def _dec_scores(q, nq, nk, kt, knew, dil, sink_col):
    rep = nq // nk
    w = kt.shape[1]
    qrows = jnp.concatenate([q[:, h * HD:(h + 1) * HD] for h in range(nq)], axis=0)
    qt = jnp.concatenate([qrows] * nk, axis=1) if nk > 1 else qrows
    r_i = lax.broadcasted_iota(I32, qt.shape, 0)
    l_i = lax.broadcasted_iota(I32, qt.shape, 1)
    qbd = jnp.where((r_i // (S * rep)) == (l_i // HD), qt, 0.0)
    scale = 1.0 / math.sqrt(HD)
    sc = _dot(qbd, kt) * scale
    tok = lax.broadcasted_iota(I32, sc.shape, 0) % S
    back = lax.broadcasted_iota(I32, sc.shape, 1) - tok
    sc = jnp.where((back >= 0) & ((back & (dil - 1)) == 0), sc, -jnp.inf)
    sn = _dot_nt(qbd, knew) * scale
    back = lax.broadcasted_iota(I32, sn.shape, 0) % S - lax.broadcasted_iota(I32, sn.shape, 1)
    sn = jnp.where((back >= 0) & ((back & (dil - 1)) == 0), sn, -jnp.inf)
    m = jnp.maximum(jnp.max(sc, axis=1, keepdims=True), jnp.max(sn, axis=1, keepdims=True))
    if sink_col is not None:
        m = jnp.maximum(m, sink_col)
    ec, en = jnp.exp(sc - m), jnp.exp(sn - m)
    den = jnp.sum(ec, axis=1, keepdims=True) + jnp.sum(en, axis=1, keepdims=True)
    if sink_col is not None:
        den = den + jnp.exp(sink_col - m)
    return ec, en, m, den


def _pick_heads(o_all, nq, nk):
    rep = nq // nk
    cols = []
    for h in range(nq):
        kh = h // rep
        cols.append(o_all[h * S:(h + 1) * S, kh * HD:(kh + 1) * HD])
    return jnp.concatenate(cols, axis=1)


def _shift_cache(new_ref, old, kv_new):
    rows, w = old.shape
    nblk = w // BLK
    keep = BLK - S
    lane = lax.broadcasted_iota(I32, (rows, BLK), 1)
    left = [pltpu.roll(old[:, j * BLK:(j + 1) * BLK], keep, axis=1) for j in range(nblk)]
    tail = jnp.concatenate([jnp.zeros((keep, rows), F32), kv_new], axis=0).T
    left.append(tail)
    shape = new_ref.shape[:-1] + (BLK,)
    for j in range(nblk):
        new_ref[:, :, :, j * BLK:(j + 1) * BLK] = jnp.where(lane < keep, left[j], left[j + 1]).reshape(shape)


def _decode_attn_kernel(sink_ref, qa_ref, qc_ref, c1_ref, c2_ref, c3_ref, cc_ref, *rest):
    n_alias = len(rest) - 6
    n1_ref, n2_ref, n3_ref, nc_ref, oa_ref, oc_ref = rest[n_alias:]
    qa = qa_ref[...]
    qc = qc_ref[...]
    caches = (c1_ref, c2_ref, c3_ref)
    news = (n1_ref, n2_ref, n3_ref)
    grp = []
    for gi, (w, dil) in enumerate(A_PATTERN):
        q = qa[:, gi * A_W:(gi + 1) * A_W]
        knew = qa[:, 768 + gi * A_W:768 + (gi + 1) * A_W]
        vnew = qa[:, 1536 + gi * A_W:1536 + (gi + 1) * A_W]
        old = caches[gi][...].reshape(2 * A_W, w)
        ec, en, m, den = _dec_scores(q, A_HEADS, A_HEADS, old[0:A_W], knew, dil, None)
        grp.append((ec, en, m + jnp.log(den), den, old[A_W:2 * A_W], vnew))
        _shift_cache(news[gi], old, jnp.concatenate([knew, vnew], axis=1))
    lmax = functools.reduce(jnp.maximum, [g[2] for g in grp])
    gws = [jnp.exp(g[2] - lmax) for g in grp]
    gtot = functools.reduce(jnp.add, gws)
    o_all = None
    for gi, (ec, en, _, den, vt, vnew) in enumerate(grp):
        wcol = (gws[gi] / gtot) / den
        t = _dot_nt(ec * wcol, vt) + _dot(en * wcol, vnew)
        o_all = t if o_all is None else o_all + t
    oa_ref[...] = _pick_heads(o_all, A_HEADS, A_HEADS).astype(oa_ref.dtype)
    nqc = C_HEADS * HD
    kvw = C_KV * HD
    old = cc_ref[...].reshape(2 * kvw, BLK)
    knew = qc[:, nqc:nqc + kvw]
    vnew = qc[:, nqc + kvw:nqc + 2 * kvw]
    ec, en, m, den = _dec_scores(qc[:, 0:nqc], C_HEADS, C_KV, old[0:kvw], knew, 1, sink_ref[...])
    o_c = _dot_nt(ec / den, old[kvw:2 * kvw]) + _dot(en / den, vnew)
    oc_ref[...] = _pick_heads(o_c, C_HEADS, C_KV).astype(oc_ref.dtype)
    _shift_cache(nc_ref, old, qc[:, nqc:nqc + 2 * kvw])


def _decode_attn(layer, sink_col, qkv_a, qkv_c, caches, prev_out):
    row0 = NP // S

    def cache_spec(c):
        return pl.BlockSpec((None, None) + c.shape[2:], lambda b: (layer, b, 0, 0, 0, 0))

    in_specs = [pl.BlockSpec((C_HEADS * S, 1), lambda b: (0, 0)),
                pl.BlockSpec((S, QKV_A), lambda b: (row0 + b, 0)),
                pl.BlockSpec((S, QKV_C), lambda b: (row0 + b, 0))] + [cache_spec(c) for c in caches]
    args = [sink_col, qkv_a, qkv_c] + list(caches)
    aliases = {}
    if prev_out is not None:
        for k, p in enumerate(prev_out):
            in_specs.append(pl.BlockSpec(memory_space=pl.ANY))
            args.append(p)
            aliases[len(args) - 1] = k
    out_specs = [cache_spec(c) for c in caches] + [
        pl.BlockSpec((S, A_W), lambda b: (b, 0)),
        pl.BlockSpec((S, C_HEADS * HD), lambda b: (b, 0))]
    out_shape = [jax.ShapeDtypeStruct(c.shape, F32) for c in caches] + [
        jax.ShapeDtypeStruct((NS, A_W), BF16), jax.ShapeDtypeStruct((NS, C_HEADS * HD), BF16)]
    return pl.pallas_call(
        _decode_attn_kernel,
        grid=(NSEQ,),
        in_specs=in_specs,
        out_specs=out_specs,
        out_shape=out_shape,
        input_output_aliases=aliases,
        compiler_params=_cp(("arbitrary",), 56),
        name=f"decode_attn{layer}",
    )(*args)


def _ssd_kernel(xbcdt_ref, z_ref, conv0_ref, h0_ref, convw_ref, convb_ref, dtb_ref, arow_ref,
                dskip_ref, nrm_ref, *rest, q_len):
    y_ref, hout_ref, xp_ref, h_ref, yacc_ref = rest[-5:]
    c = pl.program_id(1)
    Q = q_len

    @pl.when(c == 0)
    def _():
        xp_ref[0:8, :] = conv0_ref[...]
        h_ref[...] = h0_ref[...]

    xbc = xbcdt_ref[:, 0:CONV_DIM]
    xp_ref[8:8 + Q, :] = xbc
    w = convw_ref[...]
    acc = convb_ref[...] + w[3:4, :] * xbc
    acc = acc + w[2:3, :] * xp_ref[7:7 + Q, :]
    acc = acc + w[1:2, :] * xp_ref[6:6 + Q, :]
    acc = acc + w[0:1, :] * xp_ref[5:5 + Q, :]
    u = _silu(acc)
    xp_ref[0:8, :] = xbc[Q - 8:Q, :]

    dtp = _softplus(xbcdt_ref[:, CONV_DIM:XBCDT_W] + dtb_ref[...])
    da = dtp * arow_ref[...]
    ri = lax.broadcasted_iota(I32, (Q, Q), 0)
    ci = lax.broadcasted_iota(I32, (Q, Q), 1)
    tri = ri >= ci
    acs = jnp.dot(tri.astype(F32), da, precision=lax.Precision.HIGHEST, preferred_element_type=F32)
    acs_t = acs.T
    dt_t = dtp.T
    hpg = SSM_HEADS // 2
    for g in range(2):
        bg = u[:, SSM_INNER + g * SSM_N:SSM_INNER + (g + 1) * SSM_N]
        cg = u[:, SSM_INNER + (2 + g) * SSM_N:SSM_INNER + (3 + g) * SSM_N]
        cb = _dot_nt(cg, bg)
        for hh in range(hpg):
            hd = g * hpg + hh
            a_col = acs[:, hd:hd + 1]
            seg = a_col - acs_t[hd:hd + 1, :]
            decay = jnp.exp(jnp.where(tri, seg, -jnp.inf))
            mh = cb * decay * dt_t[hd:hd + 1, :]
            xh = u[:, hd * HD:(hd + 1) * HD]
            hst = h_ref[hd]
            y = _dot(mh, xh) + _dot_nt(cg, hst) * jnp.exp(a_col)
            a_last = acs[Q - 1:Q, hd:hd + 1]
            tail = jnp.exp(a_last - a_col) * dtp[:, hd:hd + 1]
            h_ref[hd] = hst * jnp.exp(a_last) + _dot_tn(xh * tail, bg)
            yacc_ref[:, hd * HD:(hd + 1) * HD] = y + dskip_ref[:, hd * HD:(hd + 1) * HD] * xh
    yz = yacc_ref[...] * _silu(z_ref[...])
    half = SSM_INNER // 2
    for g in range(2):
        yg = yz[:, g * half:(g + 1) * half]
        y_ref[:, g * half:(g + 1) * half] = (_rms(yg) * nrm_ref[:, g * half:(g + 1) * half]).astype(y_ref.dtype)

    @pl.when(c == pl.num_programs(1) - 1)
    def _():
        hout_ref[...] = h_ref[...]


def _ssd(layer, xbcdt, z, conv0, h0, convw, convb, dtb, arow, dskip, nrm, nseq, q_len, row0, h_prev=None):
    nchunks = (NP if nseq == 1 else S) // q_len
    rows = nseq * nchunks * q_len
    vec = lambda n: pl.BlockSpec((1, n), lambda b, c: (0, 0))
    state = pl.BlockSpec((None, None, SSM_HEADS, HD, SSM_N), lambda b, c: (layer, b, 0, 0, 0))
    in_specs = [pl.BlockSpec((q_len, XBCDT_W), lambda b, c: (row0 + b * nchunks + c, 0)),
                pl.BlockSpec((q_len, SSM_INNER), lambda b, c: (row0 + b * nchunks + c, 0)),
                pl.BlockSpec((None, 8, CONV_DIM), lambda b, c: (b, 0, 0)),
                state,
                pl.BlockSpec((4, CONV_DIM), lambda b, c: (0, 0)),
                vec(CONV_DIM), vec(128), vec(128), vec(SSM_INNER), vec(SSM_INNER)]
    args = [xbcdt, z, conv0, h0, convw, convb, dtb, arow, dskip, nrm]
    aliases = {}
    if h_prev is not None:
        in_specs.append(pl.BlockSpec(memory_space=pl.ANY))
        args.append(h_prev)
        aliases[len(args) - 1] = 1
    return pl.pallas_call(
        functools.partial(_ssd_kernel, q_len=q_len),
        grid=(nseq, nchunks),
        in_specs=in_specs,
        out_specs=[pl.BlockSpec((q_len, SSM_INNER), lambda b, c: (b * nchunks + c, 0)), state],
        out_shape=[jax.ShapeDtypeStruct((rows, SSM_INNER), BF16), jax.ShapeDtypeStruct(h0.shape, F32)],
        scratch_shapes=[pltpu.VMEM((q_len + 8, CONV_DIM), F32),
                        pltpu.VMEM((SSM_HEADS, HD, SSM_N), F32),
                        pltpu.VMEM((q_len, SSM_INNER), F32)],
        input_output_aliases=aliases,
        compiler_params=_cp(("arbitrary", "arbitrary")),
        name=f"ssd_q{q_len}",
    )(*args)


def _merge_kernel(h_ref, oa_ref, ob_ref, oc_ref, g0_ref, g1_ref, g2_ref, wa_ref, wb_ref, wc_ref, o_ref):
    h = h_ref[...]
    acc = _sigmoid(_dot(h, g0_ref[...])) * _dot(oa_ref[...], wa_ref[...])
    acc = acc + _sigmoid(_dot(h, g1_ref[...])) * _dot(ob_ref[...], wb_ref[...])
    acc = acc + _sigmoid(_dot(h, g2_ref[...])) * _dot(oc_ref[...], wc_ref[...])
    o_ref[...] = acc.astype(o_ref.dtype)


def _merge(layer, h1, oa, ob, oc, wgate, w_br_a, w_br_b, w_br_c):
    tm, tn = 1024, 512
    nj = D // tn
    row = lambda k: pl.BlockSpec((tm, k), lambda i, j: (i, 0))
    gate = lambda b: pl.BlockSpec((D, tn), lambda i, j: (0, b * nj + j))
    br = lambda k: pl.BlockSpec((None, k, tn), lambda i, j: (layer, 0, j))
    return pl.pallas_call(
        _merge_kernel,
        grid=(T // tm, nj),
        in_specs=[row(D), row(A_W), row(SSM_INNER), row(C_HEADS * HD), gate(0), gate(1), gate(2),
                  br(A_W), br(SSM_INNER), br(C_HEADS * HD)],
        out_specs=pl.BlockSpec((tm, tn), lambda i, j: (i, j)),
        out_shape=jax.ShapeDtypeStruct((T, D), BF16),
        compiler_params=_cp(("parallel", "arbitrary"), 56),
        name="merge",
    )(h1, oa, ob, oc, wgate, wgate, wgate, w_br_a, w_br_b, w_br_c)


def _outproj_kernel(m_ref, w_ref, x_ref, gpost_ref, gpre_ref, gap_ref, gas_ref, shp_ref, shs_ref,
                    scp_ref, scs_ref, x_out_ref, hf_ref, hp_ref):
    is_p = pl.program_id(0) < NPT_TOK
    ga = _sel_tile(is_p, gap_ref, gas_ref)
    sh = _sel_tile(is_p, shp_ref, shs_ref)
    sc = _sel_tile(is_p, scp_ref, scs_ref)
    mix = _dot(m_ref[...], w_ref[...])
    xn = x_ref[...] + ga * (_rms(mix) * gpost_ref[...])
    x_out_ref[...] = xn
    hf = (_rms(xn) * gpre_ref[...]) * (1.0 + sc) + sh
    hf_ref[...] = hf
    _store_token_tiles(hp_ref, _pack_bf16_pairs(hf))


def _outproj(merged, w_out_bf, x, g_post, g_pre, mod_p, mod_s):
    tok = pl.BlockSpec((TM_TOK, D), lambda i: (i, 0))
    vec = pl.BlockSpec((1, D), lambda i: (0, 0))
    return pl.pallas_call(
        _outproj_kernel,
        grid=(T // TM_TOK,),
        in_specs=[tok, pl.BlockSpec((D, D), lambda i: (0, 0)), tok, vec, vec]
        + _mod_specs(2) + _mod_specs(3) + _mod_specs(4),
        out_specs=[tok, tok, pl.BlockSpec((TM_TOK * XS_R, LANES), lambda i: (i, 0))],
        out_shape=[jax.ShapeDtypeStruct((T, D), F32)] * 2
        + [jax.ShapeDtypeStruct((T * XS_R, LANES), jnp.uint32)],
        compiler_params=_cp(("parallel",), 56),
        name="outproj",
    )(merged, w_out_bf, x, g_post.reshape(1, D), g_pre.reshape(1, D), mod_p, mod_s, mod_p, mod_s, mod_p, mod_s)


def _router_kernel(hf_ref, rwt_ref, bias_ref, topi_ref, topw_ref):
    tm = hf_ref.shape[0]
    logits = lax.dot_general(rwt_ref[...], hf_ref[...], (((1,), (1,)), ((), ())),
                             precision=lax.Precision.HIGHEST, preferred_element_type=F32)
    scores = _sigmoid(logits)
    sel = scores + bias_ref[...]
    ng = 8
    sel3 = sel.reshape(ng, N_EXP // ng, tm)
    idx3 = lax.broadcasted_iota(I32, sel3.shape, 1)
    m1 = jnp.max(sel3, axis=1, keepdims=True)
    first = jnp.min(jnp.where(sel3 == m1, idx3, N_EXP), axis=1, keepdims=True)
    m2 = jnp.max(jnp.where(idx3 == first, -jnp.inf, sel3), axis=1, keepdims=True)
    gs = m1 + m2
    gidx = lax.broadcasted_iota(I32, (ng, 1, tm), 0)
    gmask = jnp.zeros((ng, 1, tm), jnp.bool_)
    for _ in range(4):
        m = jnp.max(gs, axis=0, keepdims=True)
        f = jnp.min(jnp.where(gs == m, gidx, ng), axis=0, keepdims=True)
        hit = gidx == f
        gmask = jnp.logical_or(gmask, hit)
        gs = jnp.where(hit, -jnp.inf, gs)
    masked = jnp.where(gmask, sel3, -jnp.inf).reshape(N_EXP, tm)
    eidx = lax.broadcasted_iota(I32, (N_EXP, tm), 0)
    tis, tws = [], []
    for _ in range(TOP_K):
        m = jnp.max(masked, axis=0, keepdims=True)
        f = jnp.min(jnp.where(masked == m, eidx, N_EXP), axis=0, keepdims=True)
        hit = eidx == f
        tis.append(f)
        tws.append(jnp.sum(jnp.where(hit, scores, 0.0), axis=0, keepdims=True))
        masked = jnp.where(hit, -jnp.inf, masked)
    wsum = functools.reduce(jnp.add, tws)
    topi_ref[...] = jnp.concatenate(tis, axis=0)
    topw_ref[...] = jnp.concatenate([w / wsum * ROUTED_SCALE for w in tws], axis=0)


def _router(hf, rwt, bias_col):
    tm = 512
    return pl.pallas_call(
        _router_kernel,
        grid=(T // tm,),
        in_specs=[pl.BlockSpec((tm, D), lambda i: (i, 0)),
                  pl.BlockSpec((N_EXP, D), lambda i: (0, 0)),
                  pl.BlockSpec((N_EXP, 1), lambda i: (0, 0))],
        out_specs=[pl.BlockSpec((TOP_K, tm), lambda i: (0, i))] * 2,
        out_shape=[jax.ShapeDtypeStruct((TOP_K, T), I32), jax.ShapeDtypeStruct((TOP_K, T), F32)],
        compiler_params=_cp(("parallel",)),
        name="router",
    )(hf, rwt, bias_col)


def _route_tables(topi):
    onehot = topi[None] == jnp.arange(N_EXP, dtype=I32)[:, None, None]
    tok = jnp.any(onehot, axis=1).astype(I32)
    incl = jnp.cumsum(tok, axis=1)
    counts = incl[:, -1]
    padded = (counts + BM - 1) // BM * BM
    pad_end = jnp.cumsum(padded)
    base = (pad_end - padded)[:, None] + incl - tok
    dest = jnp.sum(jnp.where(onehot, base[:, None, :], 0), axis=0).astype(I32)
    starts = jnp.arange(NB, dtype=I32) * BM
    block_e = jnp.minimum(jnp.sum(pad_end[None, :] <= starts[:, None], axis=1), N_EXP - 1).astype(I32)
    n_used = (pad_end[-1] // BM).astype(I32).reshape(1)
    prev_e = jnp.concatenate([jnp.full((1,), -1, I32), block_e[:-1]])
    first = (block_e != prev_e).astype(I32)
    eid = jnp.arange(N_EXP, dtype=I32)
    later = jnp.where((counts[None, :] > 0) & (eid[None, :] > eid[:, None]), eid[None, :], N_EXP)
    nxt = jnp.min(later, axis=1)
    nxt = jnp.where(nxt == N_EXP, -1, nxt).astype(I32)
    next_e = jnp.sum(jnp.where(block_e[:, None] == eid[None, :], nxt[None, :], 0), axis=1).astype(I32)
    return dest, block_e, n_used, first, next_e


def _dispatch_kernel(dest_ref, hf_ref, xs_in_ref, xs_ref, sem):
    del xs_in_ref

    def body(t, carry):
        for k in range(TOP_K):
            src = hf_ref.at[pl.ds(pl.multiple_of(t * XS_R, XS_R), XS_R)]
            dst = xs_ref.at[pl.ds(pl.multiple_of(dest_ref[k, t] * XS_R, XS_R), XS_R)]
            pltpu.make_async_copy(src, dst, sem).start(priority=k % 2)
        return carry

    lax.fori_loop(0, TT_DISP, body, 0)
    for k in range(TOP_K):
        pltpu.make_async_copy(hf_ref, xs_ref.at[pl.ds(0, TT_DISP * XS_R)], sem).wait()


def _dispatch(dest, hf, xs_buf):
    return pl.pallas_call(
        _dispatch_kernel,
        grid=(T // TT_DISP,),
        in_specs=[pl.BlockSpec((TOP_K, TT_DISP), lambda i: (0, i), memory_space=pltpu.SMEM),
                  pl.BlockSpec((TT_DISP * XS_R, LANES), lambda i: (i, 0)),
                  pl.BlockSpec(memory_space=pl.ANY)],
        out_specs=pl.BlockSpec(memory_space=pl.ANY),
        out_shape=jax.ShapeDtypeStruct((P_ROWS * XS_R, LANES), jnp.uint32),
        scratch_shapes=[pltpu.SemaphoreType.DMA(())],
        input_output_aliases={2: 0},
        compiler_params=_cp(("arbitrary",)),
        name="moe_dispatch",
    )(dest, hf, xs_buf)


def _swiglu(x, wg, wu, wd):
    return _dot(_silu(_dot(x, wg)) * _dot(x, wu), wd)


def _expert_kernel(be_ref, nu_ref, first_ref, next_ref, x_ref, wg_hbm, wu_hbm, wd_hbm, o_ref,
                   stage_g, stage_u, stage_d, wg_bf, wu_bf, wd_bf, sems, *, layer):
    i = pl.program_id(0)
    used = i < nu_ref[0]

    def weight_copies(e):
        return [pltpu.make_async_copy(wg_hbm.at[layer, e], stage_g, sems.at[0]),
                pltpu.make_async_copy(wu_hbm.at[layer, e], stage_u, sems.at[1]),
                pltpu.make_async_copy(wd_hbm.at[layer, e], stage_d, sems.at[2])]

    @pl.when(i == 0)
    def _():
        for c in weight_copies(be_ref[0]):
            c.start()

    @pl.when(jnp.logical_and(used, first_ref[i] == 1))
    def _():
        for c in weight_copies(be_ref[i]):
            c.wait()
        wg_bf[...] = stage_g[...].astype(BF16)
        wu_bf[...] = stage_u[...].astype(BF16)
        wd_bf[...] = stage_d[...].astype(BF16)

        @pl.when(next_ref[i] >= 0)
        def _():
            for c in weight_copies(next_ref[i]):
                c.start()

    @pl.when(used)
    def _():
        x = _unpack_bf16_pairs(_load_token_tiles(x_ref, (), BM, XS_R))
        o_ref[...] = _swiglu(x, wg_bf[...], wu_bf[...], wd_bf[...])

    @pl.when(jnp.logical_not(used))
    def _():
        o_ref[...] = jnp.zeros_like(o_ref)


def _experts(layer, xs, block_e, n_used, first, next_e, exp_gate, exp_up, exp_down):
    hbm = pl.BlockSpec(memory_space=pl.ANY)
    return pl.pallas_call(
        functools.partial(_expert_kernel, layer=layer),
        grid_spec=pltpu.PrefetchScalarGridSpec(
            num_scalar_prefetch=4,
            grid=(NB,),
            in_specs=[pl.BlockSpec((BM * XS_R, LANES), lambda i, *_: (i, 0)), hbm, hbm, hbm],
            out_specs=pl.BlockSpec((BM, D), lambda i, *_: (i, 0)),
            scratch_shapes=[pltpu.VMEM((D, MOE_H), F32), pltpu.VMEM((D, MOE_H), F32),
                            pltpu.VMEM((MOE_H, D), F32),
                            pltpu.VMEM((D, MOE_H), BF16), pltpu.VMEM((D, MOE_H), BF16),
                            pltpu.VMEM((MOE_H, D), BF16),
                            pltpu.SemaphoreType.DMA((3,))]),
        out_shape=jax.ShapeDtypeStruct((P_ROWS, D), F32),
        compiler_params=_cp(("arbitrary",), 48),
        name="moe_experts",
    )(block_e, n_used, first, next_e, xs, exp_gate, exp_up, exp_down)


def _shared_kernel(x_ref, wg_ref, wu_ref, wd_ref, o_ref):
    o_ref[...] = _swiglu(x_ref[...], wg_ref[...], wu_ref[...], wd_ref[...])


def _shared_expert(layer, hf, wg, wu, wd):
    tm = 512
    wspec = lambda a, b: pl.BlockSpec((None, a, b), lambda i: (layer, 0, 0))
    return pl.pallas_call(
        _shared_kernel,
        grid=(T // tm,),
        in_specs=[pl.BlockSpec((tm, D), lambda i: (i, 0)), wspec(D, MOE_H), wspec(D, MOE_H), wspec(MOE_H, D)],
        out_specs=pl.BlockSpec((tm, D), lambda i: (i, 0)),
        out_shape=jax.ShapeDtypeStruct((T, D), F32),
        compiler_params=_cp(("parallel",), 56),
        name="moe_shared",
    )(hf, wg, wu, wd)


def _moe_combine_kernel(dest_ref, dnext_ref, ys_ref, w_ref, fsh_ref, x_ref, gpost_ref, gap_ref, gas_ref,
                        o_ref, gbuf, sems):
    tt = TT_DISP
    i = pl.program_id(0)
    is_p = i < NP // tt
    slot = i % 2

    def gather(d_ref, s):
        def body(t, carry):
            for k in range(TOP_K):
                pltpu.make_async_copy(ys_ref.at[pl.ds(d_ref[k, t], 1)], gbuf.at[s, k, pl.ds(t, 1)],
                                      sems.at[s]).start(priority=k % 2)
            return carry
        lax.fori_loop(0, tt, body, 0)

    @pl.when(i == 0)
    def _():
        gather(dest_ref, 0)

    @pl.when(i + 1 < pl.num_programs(0))
    def _():
        gather(dnext_ref, 1 - slot)

    for k in range(TOP_K):
        pltpu.make_async_copy(ys_ref.at[pl.ds(0, tt)], gbuf.at[slot, k], sems.at[slot]).wait()
    w = w_ref[...]
    f = fsh_ref[...]
    for k in range(TOP_K):
        f = f + gbuf[slot, k] * w[:, k:k + 1]
    ga = jnp.where(is_p, gap_ref[0:1, :], gas_ref[...])
    o_ref[...] = x_ref[...] + ga * (_rms(f) * gpost_ref[...])


def _moe_combine(dest, ys, topw_t, f_shared, x, g_post, mod_p, mod_s):
    tt = TT_DISP
    tok = pl.BlockSpec((tt, D), lambda i: (i, 0))
    return pl.pallas_call(
        _moe_combine_kernel,
        grid=(T // tt,),
        in_specs=[pl.BlockSpec((TOP_K, tt), lambda i: (0, i), memory_space=pltpu.SMEM),
                  pl.BlockSpec((TOP_K, tt), lambda i: (0, jnp.minimum(i + 1, T // tt - 1)),
                               memory_space=pltpu.SMEM),
                  pl.BlockSpec(memory_space=pl.ANY),
                  pl.BlockSpec((tt, TOP_K), lambda i: (i, 0)),
                  tok, tok, pl.BlockSpec((1, D), lambda i: (0, 0)),
                  pl.BlockSpec((8, D), lambda i: (0, 5)),
                  pl.BlockSpec((tt, D), lambda i: (jnp.maximum(i - NP // tt, 0), 5))],
        out_specs=tok,
        out_shape=jax.ShapeDtypeStruct((T, D), F32),
        scratch_shapes=[pltpu.VMEM((2, TOP_K, tt, D), F32), pltpu.SemaphoreType.DMA((2,))],
        compiler_params=_cp(("arbitrary",)),
        name="moe_combine",
    )(dest, dest, ys, topw_t, f_shared, x, g_post.reshape(1, D), mod_p, mod_s)


def _pad_cols(w, n):
    return jnp.pad(w, ((0, 0), (0, n - w.shape[1])))


def _layer(l, x, mod_p, mod_s, p, dec_caches, dec_prev, ssm_prev, xs_buf):
    w_in = p['w_in'][l]
    h1 = _prenorm(x, p['norm_pre_mix'][l], mod_p, mod_s)
    qkv_a = _matmul(h1, w_in[:, OFF_QKV_A:OFF_QKV_C].astype(BF16), 1024, 768, "proj_qkv_a")
    qkv_c = _matmul(h1, w_in[:, OFF_QKV_C:OFF_Z].astype(BF16), 1024, QKV_C, "proj_qkv_c")
    z = _matmul(h1, w_in[:, OFF_Z:OFF_XBC].astype(BF16), 1024, SSM_INNER, "proj_z")
    xbcdt = _matmul(h1, _pad_cols(w_in[:, OFF_XBC:OFF_GATES], XBCDT_W).astype(BF16), 1024, XBCDT_W, "proj_xbcdt")

    outs, lses = zip(*[_banded_a(qkv_a, gi, dil) for gi, (_, dil) in enumerate(A_PATTERN)])
    oa_p = _combine_groups(outs, lses)
    oc_p = _banded_c(qkv_c, p['sinks'][l])
    sink_col = jnp.repeat(p['sinks'][l], S).reshape(C_HEADS * S, 1)
    n1, n2, n3, ncc, oa_s, oc_s = _decode_attn(l, sink_col, qkv_a, qkv_c, dec_caches, dec_prev)

    dtb = jnp.pad(p['dt_bias'][l], (0, 128 - SSM_HEADS)).reshape(1, 128)
    arow = jnp.pad(-jnp.exp(p['a_log'][l]), (0, 128 - SSM_HEADS)).reshape(1, 128)
    dskip = jnp.repeat(p['d_skip'][l], HD).reshape(1, SSM_INNER)
    nrm = p['ssm_norm'][l].reshape(1, SSM_INNER)
    convw = p['conv_w'][l]
    convb = p['conv_b'][l].reshape(1, CONV_DIM)
    ob_p, ssm_p = _ssd(0, xbcdt, z, jnp.zeros((1, 8, CONV_DIM), F32),
                       jnp.zeros((1, 1, SSM_HEADS, HD, SSM_N), F32),
                       convw, convb, dtb, arow, dskip, nrm, 1, BLK, 0)
    conv0_s = jnp.pad(p['state_conv'][l], ((0, 0), (5, 0), (0, 0)))
    ob_s, ssm_s = _ssd(l, xbcdt, z, conv0_s, p['state_ssm'],
                       convw, convb, dtb, arow, dskip, nrm, NSEQ, S, NP // S, h_prev=ssm_prev)

    oa = jnp.concatenate([oa_p, oa_s], axis=0)
    ob = jnp.concatenate([ob_p, ob_s], axis=0)
    oc = jnp.concatenate([oc_p, oc_s], axis=0)
    merged = _merge(l, h1, oa, ob, oc, w_in[:, OFF_GATES:].astype(BF16), p['w_br_a'], p['w_br_b'], p['w_br_c'])
    x, hf, hf_packed = _outproj(merged, p['w_out'][l].astype(BF16), x, p['norm_post_mix'][l],
                                p['norm_pre_ffn'][l], mod_p, mod_s)

    topi, topw = _router(hf, p['router_w'][l].T, p['router_bias'][l].reshape(N_EXP, 1))
    dest, block_e, n_used, first, next_e = _route_tables(topi)
    xs = _dispatch(dest, hf_packed, xs_buf)
    ys = _experts(l, xs, block_e, n_used, first, next_e, p['exp_gate'], p['exp_up'], p['exp_down'])
    f_shared = _shared_expert(l, hf, p['shared_gate'], p['shared_up'], p['shared_down'])
    x = _moe_combine(dest, ys, topw.T, f_shared, x, p['norm_post_ffn'][l], mod_p, mod_s)

    def last_kv(src, k_off, v_off, rows, heads):
        k = src[NP - rows:NP, k_off:k_off + heads * HD].reshape(rows, heads, HD)
        v = src[NP - rows:NP, v_off:v_off + heads * HD].reshape(rows, heads, HD)
        return jnp.stack([k, v], axis=1)[None]

    a_states = [last_kv(qkv_a, 768 + gi * A_W, 1536 + gi * A_W, w, A_HEADS)
                for gi, (w, _) in enumerate(A_PATTERN)]
    c_state = last_kv(qkv_c, C_HEADS * HD, C_HEADS * HD + C_KV * HD, BLK, C_KV)
    conv_p = xbcdt[NP - 3:NP, :CONV_DIM][None]
    conv_s = xbcdt[NP:, :CONV_DIM].reshape(NSEQ, S, CONV_DIM)[:, S - 3:]
    prompt_state = (*a_states, c_state, ssm_p.reshape(1, 2, SSM_HEADS // 2, HD, SSM_N), conv_p)
    return x, prompt_state, (n1, n2, n3, ncc), ssm_s, conv_s, xs


def kernel(x_prompt, x_sample, cache_a1_kv, cache_a2_kv, cache_a3_kv, cache_c_kv, state_ssm, state_conv,
           c_prompt, c_sample, ada_w, ada_b, norm_pre_mix, norm_post_mix, norm_pre_ffn, norm_post_ffn,
           w_in, conv_w, conv_b, dt_bias, a_log, d_skip, ssm_norm, sinks, w_br_a, w_br_b, w_br_c, w_out,
           router_w, router_bias, exp_gate, exp_up, exp_down, shared_gate, shared_up, shared_down):
    p = dict(norm_pre_mix=norm_pre_mix, norm_post_mix=norm_post_mix, norm_pre_ffn=norm_pre_ffn,
             norm_post_ffn=norm_post_ffn, w_in=w_in, conv_w=conv_w, conv_b=conv_b, dt_bias=dt_bias,
             a_log=a_log, d_skip=d_skip, ssm_norm=ssm_norm, sinks=sinks, w_br_a=w_br_a, w_br_b=w_br_b,
             w_br_c=w_br_c, w_out=w_out, router_w=router_w, router_bias=router_bias, exp_gate=exp_gate,
             exp_up=exp_up, exp_down=exp_down, shared_gate=shared_gate, shared_up=shared_up,
             shared_down=shared_down, state_conv=state_conv,
             state_ssm=state_ssm.reshape(DEPTH, NSEQ, SSM_HEADS, HD, SSM_N))
    x = jnp.concatenate([x_prompt.reshape(NP, D), x_sample.reshape(NS, D)], axis=0)
    c_all = jnp.concatenate([c_prompt, c_sample, jnp.zeros((7, D), F32)], axis=0)
    mod = _ada(c_all, ada_w, ada_b)
    dec_caches = tuple(jnp.transpose(c, (0, 1, 3, 4, 5, 2))
                       for c in (cache_a1_kv, cache_a2_kv, cache_a3_kv, cache_c_kv))
    xs_buf = jnp.zeros((P_ROWS * XS_R, LANES), jnp.uint32)
    dec_prev = ssm_s = None
    prompt_states, conv_s = [], []
    for l in range(DEPTH):
        mod_s = jnp.repeat(mod[l, 1:1 + NSEQ], S, axis=0)
        x, pst, dec_prev, ssm_s, conv_l, xs_buf = _layer(l, x, mod[l], mod_s, p, dec_caches, dec_prev,
                                                          ssm_s, xs_buf)
        prompt_states.append(pst)
        conv_s.append(conv_l)
    a1_p, a2_p, a3_p, c_p, ssm_p, conv_p = (jnp.stack(t) for t in zip(*prompt_states))
    n1, n2, n3, ncc = (jnp.transpose(c, (0, 1, 5, 2, 3, 4)) for c in dec_prev)
    y_p = x[:NP].reshape(1, NP, D)
    y_s = x[NP:].reshape(NSEQ, S, D)
    return (y_p, y_s, a1_p, n1, a2_p, n2, a3_p, n3, c_p, ncc,
            ssm_p, ssm_s.reshape(state_ssm.shape), conv_p, jnp.stack(conv_s))
```

```python
import functools
import math

import jax
import jax.numpy as jnp
from jax import lax
from jax.experimental import pallas as pl
from jax.experimental.pallas import tpu as pltpu

F32 = jnp.float32
BF16 = jnp.bfloat16
I32 = jnp.int32

D = 2048
NP = 8192
NSEQ = 128
S = 8
NS = NSEQ * S
T = NP + NS
DEPTH = 2
HD = 64
EPS = 1e-6
A_PATTERN = ((128, 1), (512, 4), (2048, 16))
A_HEADS = 4
A_W = 256
QKV_A = 2304
C_HEADS = 12
C_KV = 3
QKV_C = 1152
SSM_INNER = 1024
SSM_HEADS = 16
SSM_N = 128
CONV_DIM = 1536
XBCDT_W = 1664
N_EXP = 64
TOP_K = 8
MOE_H = 512
ROUTED_SCALE = 2.5
BLK = 128

OFF_QKV_A = 0
OFF_QKV_C = 2304
OFF_Z = 3456
OFF_XBC = 4480
OFF_DT = 6016
OFF_GATES = 6032

TM_TOK = 256
NPT_TOK = NP // TM_TOK
BM = 256
NB = T * TOP_K // BM + N_EXP
P_ROWS = NB * BM
TT_DISP = 128
LANES = 128
XS_R = D // 2 // LANES


def _cp(sem, vmem_mb=48):
    return pltpu.CompilerParams(dimension_semantics=sem, vmem_limit_bytes=vmem_mb << 20)


def _sigmoid(x):
    return 1.0 / (1.0 + jnp.exp(-x))


def _silu(x):
    return x * _sigmoid(x)


def _softplus(x):
    return jnp.maximum(x, 0.0) + jnp.log1p(jnp.exp(-jnp.abs(x)))


def _rms(x):
    return x * lax.rsqrt(jnp.mean(x * x, axis=-1, keepdims=True) + EPS)


def _dot(a, b):
    return jnp.dot(a.astype(BF16), b.astype(BF16), preferred_element_type=F32)


def _dot_nt(a, b):
    return lax.dot_general(a.astype(BF16), b.astype(BF16), (((1,), (1,)), ((), ())),
                           preferred_element_type=F32)


def _dot_tn(a, b):
    return lax.dot_general(a.astype(BF16), b.astype(BF16), (((0,), (0,)), ((), ())),
                           preferred_element_type=F32)


def _pack_bf16_pairs(x):
    half = x.shape[1] // 2
    bits = lax.bitcast_convert_type(x.astype(BF16).astype(F32), jnp.uint32)
    return (bits[:, :half] >> 16) | (bits[:, half:] & jnp.uint32(0xFFFF0000))


def _unpack_bf16_pairs(w):
    lo = lax.bitcast_convert_type(w << 16, F32)
    hi = lax.bitcast_convert_type(w & jnp.uint32(0xFFFF0000), F32)
    return jnp.concatenate([lo.astype(BF16), hi.astype(BF16)], axis=1)


def _store_token_tiles(ref, val):
    m, c = val.shape[0], val.shape[1] // LANES
    for j in range(c):
        ref[pl.ds(j, m, stride=c), :] = val[:, j * LANES:(j + 1) * LANES]


def _load_token_tiles(ref, lead, m, c):
    return jnp.concatenate([ref[(*lead, pl.ds(j, m, stride=c), slice(None))] for j in range(c)], axis=1)


def _sel_tile(is_prompt, p_ref, s_ref):
    return jnp.where(is_prompt, p_ref[0:1, :], s_ref[...])


def _ada_kernel(c_ref, w_ref, b_ref, o_ref):
    o_ref[...] = _dot(_silu(c_ref[...]), w_ref[...]) + b_ref[...]


def _ada(c_all, ada_w, ada_b):
    rows = c_all.shape[0]
    tn = 1024
    return pl.pallas_call(
        _ada_kernel,
        grid=(DEPTH, 6 * D // tn),
        in_specs=[pl.BlockSpec((rows, D), lambda l, j: (0, 0)),
                  pl.BlockSpec((None, D, tn), lambda l, j: (l, 0, j)),
                  pl.BlockSpec((None, 1, tn), lambda l, j: (l, 0, j))],
        out_specs=pl.BlockSpec((None, rows, tn), lambda l, j: (l, 0, j)),
        out_shape=jax.ShapeDtypeStruct((DEPTH, rows, 6 * D), F32),
        compiler_params=_cp(("arbitrary", "arbitrary")),
        name="ada",
    )(c_all, ada_w, ada_b.reshape(DEPTH, 1, 6 * D))


def _mod_specs(col):
    return [pl.BlockSpec((8, D), lambda i: (0, col)),
            pl.BlockSpec((TM_TOK, D), lambda i: (jnp.maximum(i - NPT_TOK, 0), col))]


def _prenorm_kernel(x_ref, g_ref, shp_ref, shs_ref, scp_ref, scs_ref, o_ref):
    is_p = pl.program_id(0) < NPT_TOK
    sh = _sel_tile(is_p, shp_ref, shs_ref)
    sc = _sel_tile(is_p, scp_ref, scs_ref)
    y = _rms(x_ref[...]) * g_ref[...]
    o_ref[...] = (y * (1.0 + sc) + sh).astype(o_ref.dtype)


def _prenorm(x, g, mod_p, mod_s):
    return pl.pallas_call(
        _prenorm_kernel,
        grid=(T // TM_TOK,),
        in_specs=[pl.BlockSpec((TM_TOK, D), lambda i: (i, 0)),
                  pl.BlockSpec((1, D), lambda i: (0, 0))] + _mod_specs(0) + _mod_specs(1),
        out_specs=pl.BlockSpec((TM_TOK, D), lambda i: (i, 0)),
        out_shape=jax.ShapeDtypeStruct((T, D), BF16),
        compiler_params=_cp(("parallel",)),
        name="prenorm",
    )(x, g.reshape(1, D), mod_p, mod_s, mod_p, mod_s)


def _mm_kernel(x_ref, w_ref, o_ref):
    o_ref[...] = _dot(x_ref[...], w_ref[...]).astype(o_ref.dtype)


def _matmul(x, w, tm, tn, name):
    m, k = x.shape
    n = w.shape[1]
    return pl.pallas_call(
        _mm_kernel,
        grid=(m // tm, n // tn),
        in_specs=[pl.BlockSpec((tm, k), lambda i, j: (i, 0)),
                  pl.BlockSpec((k, tn), lambda i, j: (0, j))],
        out_specs=pl.BlockSpec((tm, tn), lambda i, j: (i, j)),
        out_shape=jax.ShapeDtypeStruct((m, n), F32),
        compiler_params=_cp(("parallel", "arbitrary")),
        name=name,
    )(x, w)


def _banded_kernel(*refs, nh, k_off, v_off, with_sink, with_lse):
    refs = list(refs)
    sink_ref = refs.pop(0) if with_sink else None
    q_ref, kc_ref, kp_ref, vc_ref, vp_ref = refs[:5]
    o_ref = refs[5]
    lse_ref = refs[6] if with_lse else None
    has_prev = pl.program_id(1) > 0
    row = lax.broadcasted_iota(I32, (BLK, 2 * BLK), 0)
    col = lax.broadcasted_iota(I32, (BLK, 2 * BLK), 1)
    dist = row + BLK - col
    valid = (dist >= 0) & (dist <= BLK) & (has_prev | (col >= BLK))
    q = q_ref[...]
    kc, kp, vc, vp = kc_ref[...], kp_ref[...], vc_ref[...], vp_ref[...]
    for h in range(nh):
        qh = q[:, h * HD:(h + 1) * HD]
        ko, vo = k_off[h], v_off[h]
        kk = jnp.concatenate([kp[:, ko:ko + HD], kc[:, ko:ko + HD]], axis=0)
        vv = jnp.concatenate([vp[:, vo:vo + HD], vc[:, vo:vo + HD]], axis=0)
        s = _dot_nt(qh, kk) * (1.0 / math.sqrt(HD))
        s = jnp.where(valid, s, -jnp.inf)
        m = jnp.max(s, axis=-1, keepdims=True)
        if with_sink:
            sk = sink_ref[h]
            m = jnp.maximum(m, sk)
        e = jnp.exp(s - m)
        den = jnp.sum(e, axis=-1, keepdims=True)
        if with_sink:
            den = den + jnp.exp(sk - m)
        o = _dot(e / den, vv)
        o_ref[:, h * HD:(h + 1) * HD] = o.astype(o_ref.dtype)
        if with_lse:
            lse_ref[:, h * HD:(h + 1) * HD] = jnp.broadcast_to(m + jnp.log(den), (BLK, HD))


def _banded_a(qkv, gi, dil):
    cols = jnp.concatenate([qkv[:NP, o + gi * A_W:o + (gi + 1) * A_W] for o in (0, 768, 1536)], axis=1)
    view = cols.astype(BF16).reshape(NP // dil, dil * 3 * A_W)
    nb = NP // dil // BLK
    cpr = 3

    def spec(off, prev):
        off = off // 3
        if prev:
            return pl.BlockSpec((BLK, A_W), lambda r, j: (jnp.maximum(j - 1, 0), r * cpr + off))
        return pl.BlockSpec((BLK, A_W), lambda r, j: (j, r * cpr + off))

    offs = tuple(h * HD for h in range(A_HEADS))
    out_spec = pl.BlockSpec((BLK, A_W), lambda r, j: (j, r))
    o, lse = pl.pallas_call(
        functools.partial(_banded_kernel, nh=A_HEADS, k_off=offs, v_off=offs,
                          with_sink=False, with_lse=True),
        grid=(dil, nb),
        in_specs=[spec(0, False), spec(3, False), spec(3, True), spec(6, False), spec(6, True)],
        out_specs=[out_spec, out_spec],
        out_shape=[jax.ShapeDtypeStruct((NP // dil, dil * A_W), F32)] * 2,
        compiler_params=_cp(("parallel", "arbitrary")),
        name=f"banded_a{gi}",
    )(view, view, view, view, view)
    return o.reshape(NP, A_W), lse.reshape(NP, A_W)


def _banded_c(qkv, sinks):
    nq = C_HEADS * HD
    kvw = 2 * C_KV * HD
    k_off = tuple((h // 4) * HD for h in range(C_HEADS))
    v_off = tuple(C_KV * HD + (h // 4) * HD for h in range(C_HEADS))
    cur = pl.BlockSpec((BLK, kvw), lambda r, j: (j, nq // kvw))
    prev = pl.BlockSpec((BLK, kvw), lambda r, j: (jnp.maximum(j - 1, 0), nq // kvw))
    return pl.pallas_call(
        functools.partial(_banded_kernel, nh=C_HEADS, k_off=k_off, v_off=v_off,
                          with_sink=True, with_lse=False),
        grid=(1, NP // BLK),
        in_specs=[pl.BlockSpec(memory_space=pltpu.SMEM),
                  pl.BlockSpec((BLK, nq), lambda r, j: (j, 0)), cur, prev, cur, prev],
        out_specs=pl.BlockSpec((BLK, nq), lambda r, j: (j, 0)),
        out_shape=jax.ShapeDtypeStruct((T, nq), BF16),
        compiler_params=_cp(("parallel", "arbitrary")),
        name="banded_c",
    )(sinks, qkv, qkv, qkv, qkv, qkv)


def _combine_kernel(o1, o2, o3, l1, l2, l3, out_ref):
    a, b, c = l1[...], l2[...], l3[...]
    m = jnp.maximum(jnp.maximum(a, b), c)
    ea, eb, ec = jnp.exp(a - m), jnp.exp(b - m), jnp.exp(c - m)
    tot = ea + eb + ec
    out_ref[...] = ((ea / tot) * o1[...] + (eb / tot) * o2[...] + (ec / tot) * o3[...]).astype(out_ref.dtype)


def _combine_groups(outs, lses):
    tm = 512
    spec = pl.BlockSpec((tm, A_W), lambda i: (i, 0))
    return pl.pallas_call(
        _combine_kernel,
        grid=(NP // tm,),
        in_specs=[spec] * 6,
        out_specs=spec,
        out_shape=jax.ShapeDtypeStruct((T, A_W), BF16),
        compiler_params=_cp(("parallel",)),
        name="combine_groups",
    )(*outs, *lses)


def _dec_scores(q, nq, nk, kt, knew, dil, sink_col):
    rep = nq // nk
    w = kt.shape[1]
    qrows = jnp.concatenate([q[:, h * HD:(h + 1) * HD] for h in range(nq)], axis=0)
    qt = jnp.concatenate([qrows] * nk, axis=1) if nk > 1 else qrows
    r_i = lax.broadcasted_iota(I32, qt.shape, 0)
    l_i = lax.broadcasted_iota(I32, qt.shape, 1)
    qbd = jnp.where((r_i // (S * rep)) == (l_i // HD), qt, 0.0)
    scale = 1.0 / math.sqrt(HD)
    sc = _dot(qbd, kt) * scale
    tok = lax.broadcasted_iota(I32, sc.shape, 0) % S
    back = lax.broadcasted_iota(I32, sc.shape, 1) - tok
    sc = jnp.where((back >= 0) & ((back & (dil - 1)) == 0), sc, -jnp.inf)
    sn = _dot_nt(qbd, knew) * scale
    back = lax.broadcasted_iota(I32, sn.shape, 0) % S - lax.broadcasted_iota(I32, sn.shape, 1)
    sn = jnp.where((back >= 0) & ((back & (dil - 1)) == 0), sn, -jnp.inf)
    m = jnp.maximum(jnp.max(sc, axis=1, keepdims=True), jnp.max(sn, axis=1, keepdims=True))
    if sink_col is not None:
        m = jnp.maximum(m, sink_col)
    ec, en = jnp.exp(sc - m), jnp.exp(sn - m)
    den = jnp.sum(ec, axis=1, keepdims=True) + jnp.sum(en, axis=1, keepdims=True)
    if sink_col is not None:
        den = den + jnp.exp(sink_col - m)
    return ec, en, m, den


def _pick_heads(o_all, nq, nk):
    rep = nq // nk
    cols = []
    for h in range(nq):
        kh = h // rep
        cols.append(o_all[h * S:(h + 1) * S, kh * HD:(kh + 1) * HD])
    return jnp.concatenate(cols, axis=1)


def _shift_cache(new_ref, old, kv_new):
    rows, w = old.shape
    nblk = w // BLK
    keep = BLK - S
    lane = lax.broadcasted_iota(I32, (rows, BLK), 1)
    left = [pltpu.roll(old[:, j * BLK:(j + 1) * BLK], keep, axis=1) for j in range(nblk)]
    tail = jnp.concatenate([jnp.zeros((keep, rows), F32), kv_new], axis=0).T
    left.append(tail)
    shape = new_ref.shape[:-1] + (BLK,)
    for j in range(nblk):
        new_ref[:, :, :, j * BLK:(j + 1) * BLK] = jnp.where(lane < keep, left[j], left[j + 1]).reshape(shape)


def _decode_attn_kernel(sink_ref, qa_ref, qc_ref, c1_ref, c2_ref, c3_ref, cc_ref, *rest):
    n_alias = len(rest) - 6
    n1_ref, n2_ref, n3_ref, nc_ref, oa_ref, oc_ref = rest[n_alias:]
    qa = qa_ref[...]
    qc = qc_ref[...]
    caches = (c1_ref, c2_ref, c3_ref)
    news = (n1_ref, n2_ref, n3_ref)
    grp = []
    for gi, (w, dil) in enumerate(A_PATTERN):
        q = qa[:, gi * A_W:(gi + 1) * A_W]
        knew = qa[:, 768 + gi * A_W:768 + (gi + 1) * A_W]
        vnew = qa[:, 1536 + gi * A_W:1536 + (gi + 1) * A_W]
        old = caches[gi][...].reshape(2 * A_W, w)
        ec, en, m, den = _dec_scores(q, A_HEADS, A_HEADS, old[0:A_W], knew, dil, None)
        grp.append((ec, en, m + jnp.log(den), den, old[A_W:2 * A_W], vnew))
        _shift_cache(news[gi], old, jnp.concatenate([knew, vnew], axis=1))
    lmax = functools.reduce(jnp.maximum, [g[2] for g in grp])
    gws = [jnp.exp(g[2] - lmax) for g in grp]
    gtot = functools.reduce(jnp.add, gws)
    o_all = None
    for gi, (ec, en, _, den, vt, vnew) in enumerate(grp):
        wcol = (gws[gi] / gtot) / den
        t = _dot_nt(ec * wcol, vt) + _dot(en * wcol, vnew)
        o_all = t if o_all is None else o_all + t
    oa_ref[...] = _pick_heads(o_all, A_HEADS, A_HEADS).astype(oa_ref.dtype)
    nqc = C_HEADS * HD
    kvw = C_KV * HD
    old = cc_ref[...].reshape(2 * kvw, BLK)
    knew = qc[:, nqc:nqc + kvw]
    vnew = qc[:, nqc + kvw:nqc + 2 * kvw]
    ec, en, m, den = _dec_scores(qc[:, 0:nqc], C_HEADS, C_KV, old[0:kvw], knew, 1, sink_ref[...])
    o_c = _dot_nt(ec / den, old[kvw:2 * kvw]) + _dot(en / den, vnew)
    oc_ref[...] = _pick_heads(o_c, C_HEADS, C_KV).astype(oc_ref.dtype)
    _shift_cache(nc_ref, old, qc[:, nqc:nqc + 2 * kvw])


def _decode_attn(layer, sink_col, qkv_a, qkv_c, caches, prev_out, oa_full, oc_full):
    row0 = NP // S

    def cache_spec(c):
        return pl.BlockSpec((None, None) + c.shape[2:], lambda b: (layer, b, 0, 0, 0, 0))

    in_specs = [pl.BlockSpec((C_HEADS * S, 1), lambda b: (0, 0)),
                pl.BlockSpec((S, QKV_A), lambda b: (row0 + b, 0)),
                pl.BlockSpec((S, QKV_C), lambda b: (row0 + b, 0))] + [cache_spec(c) for c in caches]
    args = [sink_col, qkv_a, qkv_c] + list(caches)
    aliases = {}
    if prev_out is not None:
        for k, p in enumerate(prev_out):
            in_specs.append(pl.BlockSpec(memory_space=pl.ANY))
            args.append(p)
            aliases[len(args) - 1] = k
    for k, full in enumerate((oa_full, oc_full)):
        in_specs.append(pl.BlockSpec(memory_space=pl.ANY))
        args.append(full)
        aliases[len(args) - 1] = len(caches) + k
    out_specs = [cache_spec(c) for c in caches] + [
        pl.BlockSpec((S, A_W), lambda b: (row0 + b, 0)),
        pl.BlockSpec((S, C_HEADS * HD), lambda b: (row0 + b, 0))]
    out_shape = [jax.ShapeDtypeStruct(c.shape, F32) for c in caches] + [
        jax.ShapeDtypeStruct((T, A_W), BF16), jax.ShapeDtypeStruct((T, C_HEADS * HD), BF16)]
    return pl.pallas_call(
        _decode_attn_kernel,
        grid=(NSEQ,),
        in_specs=in_specs,
        out_specs=out_specs,
        out_shape=out_shape,
        input_output_aliases=aliases,
        compiler_params=_cp(("arbitrary",), 56),
        name=f"decode_attn{layer}",
    )(*args)


def _ssd_kernel(xbcdt_ref, z_ref, conv0_ref, h0_ref, convw_ref, convb_ref, dtb_ref, arow_ref,
                dskip_ref, nrm_ref, *rest, q_len):
    y_ref, hout_ref, xp_ref, h_ref, yacc_ref = rest[-5:]
    c = pl.program_id(1)
    Q = q_len

    @pl.when(c == 0)
    def _():
        xp_ref[0:8, :] = conv0_ref[...]
        h_ref[...] = h0_ref[...]

    xbc = xbcdt_ref[:, 0:CONV_DIM]
    xp_ref[8:8 + Q, :] = xbc
    w = convw_ref[...]
    acc = convb_ref[...] + w[3:4, :] * xbc
    acc = acc + w[2:3, :] * xp_ref[7:7 + Q, :]
    acc = acc + w[1:2, :] * xp_ref[6:6 + Q, :]
    acc = acc + w[0:1, :] * xp_ref[5:5 + Q, :]
    u = _silu(acc)
    xp_ref[0:8, :] = xbc[Q - 8:Q, :]

    dtp = _softplus(xbcdt_ref[:, CONV_DIM:XBCDT_W] + dtb_ref[...])
    da = dtp * arow_ref[...]
    ri = lax.broadcasted_iota(I32, (Q, Q), 0)
    ci = lax.broadcasted_iota(I32, (Q, Q), 1)
    tri = ri >= ci
    acs = jnp.dot(tri.astype(F32), da, precision=lax.Precision.HIGHEST, preferred_element_type=F32)
    acs_t = acs.T
    dt_t = dtp.T
    hpg = SSM_HEADS // 2
    for g in range(2):
        bg = u[:, SSM_INNER + g * SSM_N:SSM_INNER + (g + 1) * SSM_N]
        cg = u[:, SSM_INNER + (2 + g) * SSM_N:SSM_INNER + (3 + g) * SSM_N]
        cb = _dot_nt(cg, bg)
        for hh in range(hpg):
            hd = g * hpg + hh
            a_col = acs[:, hd:hd + 1]
            seg = a_col - acs_t[hd:hd + 1, :]
            decay = jnp.exp(jnp.where(tri, seg, -jnp.inf))
            mh = cb * decay * dt_t[hd:hd + 1, :]
            xh = u[:, hd * HD:(hd + 1) * HD]
            hst = h_ref[hd]
            y = _dot(mh, xh) + _dot_nt(cg, hst) * jnp.exp(a_col)
            a_last = acs[Q - 1:Q, hd:hd + 1]
            tail = jnp.exp(a_last - a_col) * dtp[:, hd:hd + 1]
            h_ref[hd] = hst * jnp.exp(a_last) + _dot_tn(xh * tail, bg)
            yacc_ref[:, hd * HD:(hd + 1) * HD] = y + dskip_ref[:, hd * HD:(hd + 1) * HD] * xh
    yz = yacc_ref[...] * _silu(z_ref[...])
    half = SSM_INNER // 2
    for g in range(2):
        yg = yz[:, g * half:(g + 1) * half]
        y_ref[:, g * half:(g + 1) * half] = (_rms(yg) * nrm_ref[:, g * half:(g + 1) * half]).astype(y_ref.dtype)

    @pl.when(c == pl.num_programs(1) - 1)
    def _():
        hout_ref[...] = h_ref[...]


def _ssd(layer, xbcdt, z, conv0, h0, convw, convb, dtb, arow, dskip, nrm, nseq, q_len, row0,
         y_prev=None, h_prev=None):
    nchunks = (NP if nseq == 1 else S) // q_len
    vec = lambda n: pl.BlockSpec((1, n), lambda b, c: (0, 0))
    state = pl.BlockSpec((None, None, SSM_HEADS, HD, SSM_N), lambda b, c: (layer, b, 0, 0, 0))
    in_specs = [pl.BlockSpec((q_len, XBCDT_W), lambda b, c: (row0 + b * nchunks + c, 0)),
                pl.BlockSpec((q_len, SSM_INNER), lambda b, c: (row0 + b * nchunks + c, 0)),
                pl.BlockSpec((None, 8, CONV_DIM), lambda b, c: (b, 0, 0)),
                state,
                pl.BlockSpec((4, CONV_DIM), lambda b, c: (0, 0)),
                vec(CONV_DIM), vec(128), vec(128), vec(SSM_INNER), vec(SSM_INNER)]
    args = [xbcdt, z, conv0, h0, convw, convb, dtb, arow, dskip, nrm]
    aliases = {}
    for out_idx, prev in ((0, y_prev), (1, h_prev)):
        if prev is not None:
            in_specs.append(pl.BlockSpec(memory_space=pl.ANY))
            args.append(prev)
            aliases[len(args) - 1] = out_idx
    return pl.pallas_call(
        functools.partial(_ssd_kernel, q_len=q_len),
        grid=(nseq, nchunks),
        in_specs=in_specs,
        out_specs=[pl.BlockSpec((q_len, SSM_INNER), lambda b, c: (row0 + b * nchunks + c, 0)), state],
        out_shape=[jax.ShapeDtypeStruct((T, SSM_INNER), BF16), jax.ShapeDtypeStruct(h0.shape, F32)],
        scratch_shapes=[pltpu.VMEM((q_len + 8, CONV_DIM), F32),
                        pltpu.VMEM((SSM_HEADS, HD, SSM_N), F32),
                        pltpu.VMEM((q_len, SSM_INNER), F32)],
        input_output_aliases=aliases,
        compiler_params=_cp(("arbitrary", "arbitrary")),
        name=f"ssd_q{q_len}",
    )(*args)


def _merge_kernel(h_ref, oa_ref, ob_ref, oc_ref, g0_ref, g1_ref, g2_ref, wa_ref, wb_ref, wc_ref, o_ref):
    h = h_ref[...]
    acc = _sigmoid(_dot(h, g0_ref[...])) * _dot(oa_ref[...], wa_ref[...])
    acc = acc + _sigmoid(_dot(h, g1_ref[...])) * _dot(ob_ref[...], wb_ref[...])
    acc = acc + _sigmoid(_dot(h, g2_ref[...])) * _dot(oc_ref[...], wc_ref[...])
    o_ref[...] = acc.astype(o_ref.dtype)


def _merge(layer, h1, oa, ob, oc, wgate, w_br_a, w_br_b, w_br_c):
    tm, tn = 1024, 512
    nj = D // tn
    row = lambda k: pl.BlockSpec((tm, k), lambda i, j: (i, 0))
    gate = lambda b: pl.BlockSpec((D, tn), lambda i, j: (0, b * nj + j))
    br = lambda k: pl.BlockSpec((None, k, tn), lambda i, j: (layer, 0, j))
    return pl.pallas_call(
        _merge_kernel,
        grid=(T // tm, nj),
        in_specs=[row(D), row(A_W), row(SSM_INNER), row(C_HEADS * HD), gate(0), gate(1), gate(2),
                  br(A_W), br(SSM_INNER), br(C_HEADS * HD)],
        out_specs=pl.BlockSpec((tm, tn), lambda i, j: (i, j)),
        out_shape=jax.ShapeDtypeStruct((T, D), BF16),
        compiler_params=_cp(("parallel", "arbitrary"), 56),
        name="merge",
    )(h1, oa, ob, oc, wgate, wgate, wgate, w_br_a, w_br_b, w_br_c)


def _outproj_kernel(m_ref, w_ref, x_ref, gpost_ref, gpre_ref, gap_ref, gas_ref, shp_ref, shs_ref,
                    scp_ref, scs_ref, x_out_ref, hf_ref, hp_ref):
    is_p = pl.program_id(0) < NPT_TOK
    ga = _sel_tile(is_p, gap_ref, gas_ref)
    sh = _sel_tile(is_p, shp_ref, shs_ref)
    sc = _sel_tile(is_p, scp_ref, scs_ref)
    mix = _dot(m_ref[...], w_ref[...])
    xn = x_ref[...] + ga * (_rms(mix) * gpost_ref[...])
    x_out_ref[...] = xn
    hf = (_rms(xn) * gpre_ref[...]) * (1.0 + sc) + sh
    hf_ref[...] = hf
    _store_token_tiles(hp_ref, _pack_bf16_pairs(hf))


def _outproj(merged, w_out_bf, x, g_post, g_pre, mod_p, mod_s):
    tok = pl.BlockSpec((TM_TOK, D), lambda i: (i, 0))
    vec = pl.BlockSpec((1, D), lambda i: (0, 0))
    return pl.pallas_call(
        _outproj_kernel,
        grid=(T // TM_TOK,),
        in_specs=[tok, pl.BlockSpec((D, D), lambda i: (0, 0)), tok, vec, vec]
        + _mod_specs(2) + _mod_specs(3) + _mod_specs(4),
        out_specs=[tok, tok, pl.BlockSpec((TM_TOK * XS_R, LANES), lambda i: (i, 0))],
        out_shape=[jax.ShapeDtypeStruct((T, D), F32)] * 2
        + [jax.ShapeDtypeStruct((T * XS_R, LANES), jnp.uint32)],
        compiler_params=_cp(("parallel",), 56),
        name="outproj",
    )(merged, w_out_bf, x, g_post.reshape(1, D), g_pre.reshape(1, D), mod_p, mod_s, mod_p, mod_s, mod_p, mod_s)


def _router_kernel(hf_ref, rwt_ref, bias_ref, topi_ref, topw_ref):
    tm = hf_ref.shape[0]
    logits = lax.dot_general(rwt_ref[...], hf_ref[...], (((1,), (1,)), ((), ())),
                             precision=lax.Precision.HIGHEST, preferred_element_type=F32)
    scores = _sigmoid(logits)
    sel = scores + bias_ref[...]
    ng = 8
    sel3 = sel.reshape(ng, N_EXP // ng, tm)
    idx3 = lax.broadcasted_iota(I32, sel3.shape, 1)
    m1 = jnp.max(sel3, axis=1, keepdims=True)
    first = jnp.min(jnp.where(sel3 == m1, idx3, N_EXP), axis=1, keepdims=True)
    m2 = jnp.max(jnp.where(idx3 == first, -jnp.inf, sel3), axis=1, keepdims=True)
    gs = m1 + m2
    gidx = lax.broadcasted_iota(I32, (ng, 1, tm), 0)
    gmask = jnp.zeros((ng, 1, tm), jnp.bool_)
    for _ in range(4):
        m = jnp.max(gs, axis=0, keepdims=True)
        f = jnp.min(jnp.where(gs == m, gidx, ng), axis=0, keepdims=True)
        hit = gidx == f
        gmask = jnp.logical_or(gmask, hit)
        gs = jnp.where(hit, -jnp.inf, gs)
    masked = jnp.where(gmask, sel3, -jnp.inf).reshape(N_EXP, tm)
    eidx = lax.broadcasted_iota(I32, (N_EXP, tm), 0)
    tis, tws = [], []
    for _ in range(TOP_K):
        m = jnp.max(masked, axis=0, keepdims=True)
        f = jnp.min(jnp.where(masked == m, eidx, N_EXP), axis=0, keepdims=True)
        hit = eidx == f
        tis.append(f)
        tws.append(jnp.sum(jnp.where(hit, scores, 0.0), axis=0, keepdims=True))
        masked = jnp.where(hit, -jnp.inf, masked)
    wsum = functools.reduce(jnp.add, tws)
    topi_ref[...] = jnp.concatenate(tis, axis=0)
    topw_ref[...] = jnp.concatenate([w / wsum * ROUTED_SCALE for w in tws], axis=0)


def _router(hf, rwt, bias_col):
    tm = 512
    return pl.pallas_call(
        _router_kernel,
        grid=(T // tm,),
        in_specs=[pl.BlockSpec((tm, D), lambda i: (i, 0)),
                  pl.BlockSpec((N_EXP, D), lambda i: (0, 0)),
                  pl.BlockSpec((N_EXP, 1), lambda i: (0, 0))],
        out_specs=[pl.BlockSpec((TOP_K, tm), lambda i: (0, i))] * 2,
        out_shape=[jax.ShapeDtypeStruct((TOP_K, T), I32), jax.ShapeDtypeStruct((TOP_K, T), F32)],
        compiler_params=_cp(("parallel",)),
        name="router",
    )(hf, rwt, bias_col)


def _route_tables(topi):
    onehot = topi[None] == jnp.arange(N_EXP, dtype=I32)[:, None, None]
    tok = jnp.any(onehot, axis=1).astype(I32)
    incl = jnp.cumsum(tok, axis=1)
    counts = incl[:, -1]
    padded = (counts + BM - 1) // BM * BM
    pad_end = jnp.cumsum(padded)
    base = (pad_end - padded)[:, None] + incl - tok
    dest = jnp.sum(jnp.where(onehot, base[:, None, :], 0), axis=0).astype(I32)
    starts = jnp.arange(NB, dtype=I32) * BM
    block_e = jnp.minimum(jnp.sum(pad_end[None, :] <= starts[:, None], axis=1), N_EXP - 1).astype(I32)
    n_used = (pad_end[-1] // BM).astype(I32).reshape(1)
    prev_e = jnp.concatenate([jnp.full((1,), -1, I32), block_e[:-1]])
    first = (block_e != prev_e).astype(I32)
    eid = jnp.arange(N_EXP, dtype=I32)
    later = jnp.where((counts[None, :] > 0) & (eid[None, :] > eid[:, None]), eid[None, :], N_EXP)
    nxt = jnp.min(later, axis=1)
    nxt = jnp.where(nxt == N_EXP, -1, nxt).astype(I32)
    next_e = jnp.sum(jnp.where(block_e[:, None] == eid[None, :], nxt[None, :], 0), axis=1).astype(I32)
    return dest, block_e, n_used, first, next_e


def _dispatch_kernel(dest_ref, hf_ref, xs_in_ref, xs_ref, sem):
    del xs_in_ref

    def body(t, carry):
        for k in range(TOP_K):
            src = hf_ref.at[pl.ds(pl.multiple_of(t * XS_R, XS_R), XS_R)]
            dst = xs_ref.at[pl.ds(pl.multiple_of(dest_ref[k, t] * XS_R, XS_R), XS_R)]
            pltpu.make_async_copy(src, dst, sem).start(priority=k % 2)
        return carry

    lax.fori_loop(0, TT_DISP, body, 0)
    for k in range(TOP_K):
        pltpu.make_async_copy(hf_ref, xs_ref.at[pl.ds(0, TT_DISP * XS_R)], sem).wait()


def _dispatch(dest, hf, xs_buf):
    return pl.pallas_call(
        _dispatch_kernel,
        grid=(T // TT_DISP,),
        in_specs=[pl.BlockSpec((TOP_K, TT_DISP), lambda i: (0, i), memory_space=pltpu.SMEM),
                  pl.BlockSpec((TT_DISP * XS_R, LANES), lambda i: (i, 0)),
                  pl.BlockSpec(memory_space=pl.ANY)],
        out_specs=pl.BlockSpec(memory_space=pl.ANY),
        out_shape=jax.ShapeDtypeStruct((P_ROWS * XS_R, LANES), jnp.uint32),
        scratch_shapes=[pltpu.SemaphoreType.DMA(())],
        input_output_aliases={2: 0},
        compiler_params=_cp(("arbitrary",)),
        name="moe_dispatch",
    )(dest, hf, xs_buf)


def _swiglu(x, wg, wu, wd):
    return _dot(_silu(_dot(x, wg)) * _dot(x, wu), wd)


def _expert_kernel(be_ref, nu_ref, first_ref, next_ref, x_ref, wg_hbm, wu_hbm, wd_hbm, o_ref,
                   stage_g, stage_u, stage_d, wg_bf, wu_bf, wd_bf, sems, *, layer):
    i = pl.program_id(0)
    used = i < nu_ref[0]

    def weight_copies(e):
        return [pltpu.make_async_copy(wg_hbm.at[layer, e], stage_g, sems.at[0]),
                pltpu.make_async_copy(wu_hbm.at[layer, e], stage_u, sems.at[1]),
                pltpu.make_async_copy(wd_hbm.at[layer, e], stage_d, sems.at[2])]

    @pl.when(i == 0)
    def _():
        for c in weight_copies(be_ref[0]):
            c.start()

    @pl.when(jnp.logical_and(used, first_ref[i] == 1))
    def _():
        for c in weight_copies(be_ref[i]):
            c.wait()
        wg_bf[...] = stage_g[...].astype(BF16)
        wu_bf[...] = stage_u[...].astype(BF16)
        wd_bf[...] = stage_d[...].astype(BF16)

        @pl.when(next_ref[i] >= 0)
        def _():
            for c in weight_copies(next_ref[i]):
                c.start()

    @pl.when(used)
    def _():
        x = _unpack_bf16_pairs(_load_token_tiles(x_ref, (), BM, XS_R))
        o_ref[...] = _swiglu(x, wg_bf[...], wu_bf[...], wd_bf[...])

    @pl.when(jnp.logical_not(used))
    def _():
        o_ref[...] = jnp.zeros_like(o_ref)


def _experts(layer, xs, block_e, n_used, first, next_e, exp_gate, exp_up, exp_down):
    hbm = pl.BlockSpec(memory_space=pl.ANY)
    return pl.pallas_call(
        functools.partial(_expert_kernel, layer=layer),
        grid_spec=pltpu.PrefetchScalarGridSpec(
            num_scalar_prefetch=4,
            grid=(NB,),
            in_specs=[pl.BlockSpec((BM * XS_R, LANES), lambda i, *_: (i, 0)), hbm, hbm, hbm],
            out_specs=pl.BlockSpec((BM, D), lambda i, *_: (i, 0)),
            scratch_shapes=[pltpu.VMEM((D, MOE_H), F32), pltpu.VMEM((D, MOE_H), F32),
                            pltpu.VMEM((MOE_H, D), F32),
                            pltpu.VMEM((D, MOE_H), BF16), pltpu.VMEM((D, MOE_H), BF16),
                            pltpu.VMEM((MOE_H, D), BF16),
                            pltpu.SemaphoreType.DMA((3,))]),
        out_shape=jax.ShapeDtypeStruct((P_ROWS, D), F32),
        compiler_params=_cp(("arbitrary",), 48),
        name="moe_experts",
    )(block_e, n_used, first, next_e, xs, exp_gate, exp_up, exp_down)


def _shared_kernel(x_ref, wg_ref, wu_ref, wd_ref, o_ref):
    o_ref[...] = _swiglu(x_ref[...], wg_ref[...], wu_ref[...], wd_ref[...])


def _shared_expert(layer, hf, wg, wu, wd):
    tm = 512
    wspec = lambda a, b: pl.BlockSpec((None, a, b), lambda i: (layer, 0, 0))
    return pl.pallas_call(
        _shared_kernel,
        grid=(T // tm,),
        in_specs=[pl.BlockSpec((tm, D), lambda i: (i, 0)), wspec(D, MOE_H), wspec(D, MOE_H), wspec(MOE_H, D)],
        out_specs=pl.BlockSpec((tm, D), lambda i: (i, 0)),
        out_shape=jax.ShapeDtypeStruct((T, D), F32),
        compiler_params=_cp(("parallel",), 56),
        name="moe_shared",
    )(hf, wg, wu, wd)


def _moe_combine_kernel(dest_ref, dnext_ref, ys_ref, w_ref, fsh_ref, x_ref, gpost_ref, gap_ref, gas_ref,
                        o_ref, gbuf, sems):
    tt = TT_DISP
    i = pl.program_id(0)
    is_p = i < NP // tt
    slot = i % 2

    def gather(d_ref, s):
        def body(t, carry):
            for k in range(TOP_K):
                pltpu.make_async_copy(ys_ref.at[pl.ds(d_ref[k, t], 1)], gbuf.at[s, k, pl.ds(t, 1)],
                                      sems.at[s]).start(priority=k % 2)
            return carry
        lax.fori_loop(0, tt, body, 0)

    @pl.when(i == 0)
    def _():
        gather(dest_ref, 0)

    @pl.when(i + 1 < pl.num_programs(0))
    def _():
        gather(dnext_ref, 1 - slot)

    for k in range(TOP_K):
        pltpu.make_async_copy(ys_ref.at[pl.ds(0, tt)], gbuf.at[slot, k], sems.at[slot]).wait()
    w = w_ref[...]
    f = fsh_ref[...]
    for k in range(TOP_K):
        f = f + gbuf[slot, k] * w[:, k:k + 1]
    ga = jnp.where(is_p, gap_ref[0:1, :], gas_ref[...])
    o_ref[...] = x_ref[...] + ga * (_rms(f) * gpost_ref[...])


def _moe_combine(dest, ys, topw_t, f_shared, x, g_post, mod_p, mod_s):
    tt = TT_DISP
    tok = pl.BlockSpec((tt, D), lambda i: (i, 0))
    return pl.pallas_call(
        _moe_combine_kernel,
        grid=(T // tt,),
        in_specs=[pl.BlockSpec((TOP_K, tt), lambda i: (0, i), memory_space=pltpu.SMEM),
                  pl.BlockSpec((TOP_K, tt), lambda i: (0, jnp.minimum(i + 1, T // tt - 1)),
                               memory_space=pltpu.SMEM),
                  pl.BlockSpec(memory_space=pl.ANY),
                  pl.BlockSpec((tt, TOP_K), lambda i: (i, 0)),
                  tok, tok, pl.BlockSpec((1, D), lambda i: (0, 0)),
                  pl.BlockSpec((8, D), lambda i: (0, 5)),
                  pl.BlockSpec((tt, D), lambda i: (jnp.maximum(i - NP // tt, 0), 5))],
        out_specs=tok,
        out_shape=jax.ShapeDtypeStruct((T, D), F32),
        scratch_shapes=[pltpu.VMEM((2, TOP_K, tt, D), F32), pltpu.SemaphoreType.DMA((2,))],
        compiler_params=_cp(("arbitrary",)),
        name="moe_combine",
    )(dest, dest, ys, topw_t, f_shared, x, g_post.reshape(1, D), mod_p, mod_s)


def _pad_cols(w, n):
    return jnp.pad(w, ((0, 0), (0, n - w.shape[1])))


def _layer(l, x, mod_p, mod_s, p, dec_caches, dec_prev, ssm_prev, xs_buf):
    w_in = p['w_in'][l]
    h1 = _prenorm(x, p['norm_pre_mix'][l], mod_p, mod_s)
    qkv_a = _matmul(h1, w_in[:, OFF_QKV_A:OFF_QKV_C].astype(BF16), 1024, 768, "proj_qkv_a")
    qkv_c = _matmul(h1, w_in[:, OFF_QKV_C:OFF_Z].astype(BF16), 1024, QKV_C, "proj_qkv_c")
    z = _matmul(h1, w_in[:, OFF_Z:OFF_XBC].astype(BF16), 1024, SSM_INNER, "proj_z")
    xbcdt = _matmul(h1, _pad_cols(w_in[:, OFF_XBC:OFF_GATES], XBCDT_W).astype(BF16), 1024, XBCDT_W, "proj_xbcdt")

    outs, lses = zip(*[_banded_a(qkv_a, gi, dil) for gi, (_, dil) in enumerate(A_PATTERN)])
    oa_p = _combine_groups(outs, lses)
    oc_p = _banded_c(qkv_c, p['sinks'][l])
    sink_col = jnp.repeat(p['sinks'][l], S).reshape(C_HEADS * S, 1)
    n1, n2, n3, ncc, oa, oc = _decode_attn(l, sink_col, qkv_a, qkv_c, dec_caches, dec_prev, oa_p, oc_p)

    dtb = jnp.pad(p['dt_bias'][l], (0, 128 - SSM_HEADS)).reshape(1, 128)
    arow = jnp.pad(-jnp.exp(p['a_log'][l]), (0, 128 - SSM_HEADS)).reshape(1, 128)
    dskip = jnp.repeat(p['d_skip'][l], HD).reshape(1, SSM_INNER)
    nrm = p['ssm_norm'][l].reshape(1, SSM_INNER)
    convw = p['conv_w'][l]
    convb = p['conv_b'][l].reshape(1, CONV_DIM)
    ob_p, ssm_p = _ssd(0, xbcdt, z, jnp.zeros((1, 8, CONV_DIM), F32),
                       jnp.zeros((1, 1, SSM_HEADS, HD, SSM_N), F32),
                       convw, convb, dtb, arow, dskip, nrm, 1, BLK, 0)
    conv0_s = jnp.pad(p['state_conv'][l], ((0, 0), (5, 0), (0, 0)))
    ob, ssm_s = _ssd(l, xbcdt, z, conv0_s, p['state_ssm'],
                     convw, convb, dtb, arow, dskip, nrm, NSEQ, S, NP // S, y_prev=ob_p, h_prev=ssm_prev)
    merged = _merge(l, h1, oa, ob, oc, w_in[:, OFF_GATES:].astype(BF16), p['w_br_a'], p['w_br_b'], p['w_br_c'])
    x, hf, hf_packed = _outproj(merged, p['w_out'][l].astype(BF16), x, p['norm_post_mix'][l],
                                p['norm_pre_ffn'][l], mod_p, mod_s)

    topi, topw = _router(hf, p['router_w'][l].T, p['router_bias'][l].reshape(N_EXP, 1))
    dest, block_e, n_used, first, next_e = _route_tables(topi)
    xs = _dispatch(dest, hf_packed, xs_buf)
    ys = _experts(l, xs, block_e, n_used, first, next_e, p['exp_gate'], p['exp_up'], p['exp_down'])
    f_shared = _shared_expert(l, hf, p['shared_gate'], p['shared_up'], p['shared_down'])
    x = _moe_combine(dest, ys, topw.T, f_shared, x, p['norm_post_ffn'][l], mod_p, mod_s)

    def last_kv(src, k_off, v_off, rows, heads):
        k = src[NP - rows:NP, k_off:k_off + heads * HD].reshape(rows, heads, HD)
        v = src[NP - rows:NP, v_off:v_off + heads * HD].reshape(rows, heads, HD)
        return jnp.stack([k, v], axis=1)[None]

    a_states = [last_kv(qkv_a, 768 + gi * A_W, 1536 + gi * A_W, w, A_HEADS)
                for gi, (w, _) in enumerate(A_PATTERN)]
    c_state = last_kv(qkv_c, C_HEADS * HD, C_HEADS * HD + C_KV * HD, BLK, C_KV)
    conv_p = xbcdt[NP - 3:NP, :CONV_DIM][None]
    conv_s = xbcdt[NP:, :CONV_DIM].reshape(NSEQ, S, CONV_DIM)[:, S - 3:]
    prompt_state = (*a_states, c_state, ssm_p.reshape(1, 2, SSM_HEADS // 2, HD, SSM_N), conv_p)
    return x, prompt_state, (n1, n2, n3, ncc), ssm_s, conv_s, xs


def kernel(x_prompt, x_sample, cache_a1_kv, cache_a2_kv, cache_a3_kv, cache_c_kv, state_ssm, state_conv,
           c_prompt, c_sample, ada_w, ada_b, norm_pre_mix, norm_post_mix, norm_pre_ffn, norm_post_ffn,
           w_in, conv_w, conv_b, dt_bias, a_log, d_skip, ssm_norm, sinks, w_br_a, w_br_b, w_br_c, w_out,
           router_w, router_bias, exp_gate, exp_up, exp_down, shared_gate, shared_up, shared_down):
    p = dict(norm_pre_mix=norm_pre_mix, norm_post_mix=norm_post_mix, norm_pre_ffn=norm_pre_ffn,
             norm_post_ffn=norm_post_ffn, w_in=w_in, conv_w=conv_w, conv_b=conv_b, dt_bias=dt_bias,
             a_log=a_log, d_skip=d_skip, ssm_norm=ssm_norm, sinks=sinks, w_br_a=w_br_a, w_br_b=w_br_b,
             w_br_c=w_br_c, w_out=w_out, router_w=router_w, router_bias=router_bias, exp_gate=exp_gate,
             exp_up=exp_up, exp_down=exp_down, shared_gate=shared_gate, shared_up=shared_up,
             shared_down=shared_down, state_conv=state_conv,
             state_ssm=state_ssm.reshape(DEPTH, NSEQ, SSM_HEADS, HD, SSM_N))
    x = jnp.concatenate([x_prompt.reshape(NP, D), x_sample.reshape(NS, D)], axis=0)
    c_all = jnp.concatenate([c_prompt, c_sample, jnp.zeros((7, D), F32)], axis=0)
    mod = _ada(c_all, ada_w, ada_b)
    dec_caches = tuple(jnp.transpose(c, (0, 1, 3, 4, 5, 2))
                       for c in (cache_a1_kv, cache_a2_kv, cache_a3_kv, cache_c_kv))
    xs_buf = jnp.zeros((P_ROWS * XS_R, LANES), jnp.uint32)
    dec_prev = ssm_s = None
    prompt_states, conv_s = [], []
    for l in range(DEPTH):
        mod_s = jnp.repeat(mod[l, 1:1 + NSEQ], S, axis=0)
        x, pst, dec_prev, ssm_s, conv_l, xs_buf = _layer(l, x, mod[l], mod_s, p, dec_caches, dec_prev,
                                                          ssm_s, xs_buf)
        prompt_states.append(pst)
        conv_s.append(conv_l)
    a1_p, a2_p, a3_p, c_p, ssm_p, conv_p = (jnp.stack(t) for t in zip(*prompt_states))
    n1, n2, n3, ncc = (jnp.transpose(c, (0, 1, 5, 2, 3, 4)) for c in dec_prev)
    y_p = x[:NP].reshape(1, NP, D)
    y_s = x[NP:].reshape(NSEQ, S, D)
    return (y_p, y_s, a1_p, n1, a2_p, n2, a3_p, n3, c_p, ncc,
            ssm_p, ssm_s.reshape(state_ssm.shape), conv_p, jnp.stack(conv_s))
```

```python
import functools
import math

import jax
import jax.numpy as jnp
from jax import lax
from jax.experimental import pallas as pl
from jax.experimental.pallas import tpu as pltpu

F32 = jnp.float32
BF16 = jnp.bfloat16
I32 = jnp.int32

D = 2048
NP = 8192
NSEQ = 128
S = 8
NS = NSEQ * S
T = NP + NS
DEPTH = 2
HD = 64
EPS = 1e-6
A_PATTERN = ((128, 1), (512, 4), (2048, 16))
A_HEADS = 4
A_W = 256
QKV_A = 2304
C_HEADS = 12
C_KV = 3
QKV_C = 1152
SSM_INNER = 1024
SSM_HEADS = 16
SSM_N = 128
CONV_DIM = 1536
XBCDT_W = 1664
N_EXP = 64
TOP_K = 8
MOE_H = 512
ROUTED_SCALE = 2.5
BLK = 128

OFF_QKV_A = 0
OFF_QKV_C = 2304
OFF_Z = 3456
OFF_XBC = 4480
OFF_DT = 6016
OFF_GATES = 6032

TM_TOK = 256
NPT_TOK = NP // TM_TOK
BM = 256
NB = T * TOP_K // BM + N_EXP
P_ROWS = NB * BM
TT_DISP = 128
LANES = 128
XS_R = D // 2 // LANES


def _cp(sem, vmem_mb=48):
    return pltpu.CompilerParams(dimension_semantics=sem, vmem_limit_bytes=vmem_mb << 20)


def _sigmoid(x):
    return 1.0 / (1.0 + jnp.exp(-x))


def _silu(x):
    return x * _sigmoid(x)


def _softplus(x):
    return jnp.maximum(x, 0.0) + jnp.log1p(jnp.exp(-jnp.abs(x)))


def _rms(x):
    return x * lax.rsqrt(jnp.mean(x * x, axis=-1, keepdims=True) + EPS)


def _dot(a, b):
    return jnp.dot(a.astype(BF16), b.astype(BF16), preferred_element_type=F32)


def _dot_nt(a, b):
    return lax.dot_general(a.astype(BF16), b.astype(BF16), (((1,), (1,)), ((), ())),
                           preferred_element_type=F32)


def _dot_tn(a, b):
    return lax.dot_general(a.astype(BF16), b.astype(BF16), (((0,), (0,)), ((), ())),
                           preferred_element_type=F32)


def _pack_bf16_pairs(x):
    half = x.shape[1] // 2
    bits = lax.bitcast_convert_type(x.astype(BF16).astype(F32), jnp.uint32)
    return (bits[:, :half] >> 16) | (bits[:, half:] & jnp.uint32(0xFFFF0000))


def _unpack_bf16_pairs(w):
    lo = lax.bitcast_convert_type(w << 16, F32)
    hi = lax.bitcast_convert_type(w & jnp.uint32(0xFFFF0000), F32)
    return jnp.concatenate([lo.astype(BF16), hi.astype(BF16)], axis=1)


def _store_token_tiles(ref, val):
    m, c = val.shape[0], val.shape[1] // LANES
    for j in range(c):
        ref[pl.ds(j, m, stride=c), :] = val[:, j * LANES:(j + 1) * LANES]


def _load_token_tiles(ref, lead, m, c):
    return jnp.concatenate([ref[(*lead, pl.ds(j, m, stride=c), slice(None))] for j in range(c)], axis=1)


def _sel_tile(is_prompt, p_ref, s_ref):
    return jnp.where(is_prompt, p_ref[0:1, :], s_ref[...])


def _ada_kernel(c_ref, w_ref, b_ref, o_ref):
    o_ref[...] = _dot(_silu(c_ref[...]), w_ref[...]) + b_ref[...]


def _ada(c_all, ada_w, ada_b):
    rows = c_all.shape[0]
    tn = 1024
    return pl.pallas_call(
        _ada_kernel,
        grid=(DEPTH, 6 * D // tn),
        in_specs=[pl.BlockSpec((rows, D), lambda l, j: (0, 0)),
                  pl.BlockSpec((None, D, tn), lambda l, j: (l, 0, j)),
                  pl.BlockSpec((None, 1, tn), lambda l, j: (l, 0, j))],
        out_specs=pl.BlockSpec((None, rows, tn), lambda l, j: (l, 0, j)),
        out_shape=jax.ShapeDtypeStruct((DEPTH, rows, 6 * D), F32),
        compiler_params=_cp(("arbitrary", "arbitrary")),
        name="ada",
    )(c_all, ada_w, ada_b.reshape(DEPTH, 1, 6 * D))


def _mod_specs(col):
    return [pl.BlockSpec((8, D), lambda i: (0, col)),
            pl.BlockSpec((TM_TOK, D), lambda i: (jnp.maximum(i - NPT_TOK, 0), col))]


def _prenorm_kernel(x_ref, g_ref, shp_ref, shs_ref, scp_ref, scs_ref, o_ref):
    is_p = pl.program_id(0) < NPT_TOK
    sh = _sel_tile(is_p, shp_ref, shs_ref)
    sc = _sel_tile(is_p, scp_ref, scs_ref)
    y = _rms(x_ref[...]) * g_ref[...]
    o_ref[...] = (y * (1.0 + sc) + sh).astype(o_ref.dtype)


def _prenorm(x, g, mod_p, mod_s):
    return pl.pallas_call(
        _prenorm_kernel,
        grid=(T // TM_TOK,),
        in_specs=[pl.BlockSpec((TM_TOK, D), lambda i: (i, 0)),
                  pl.BlockSpec((1, D), lambda i: (0, 0))] + _mod_specs(0) + _mod_specs(1),
        out_specs=pl.BlockSpec((TM_TOK, D), lambda i: (i, 0)),
        out_shape=jax.ShapeDtypeStruct((T, D), BF16),
        compiler_params=_cp(("parallel",)),
        name="prenorm",
    )(x, g.reshape(1, D), mod_p, mod_s, mod_p, mod_s)


def _mm_kernel(x_ref, w_ref, o_ref):
    o_ref[...] = _dot(x_ref[...], w_ref[...]).astype(o_ref.dtype)


def _matmul(x, w, tm, tn, name):
    m, k = x.shape
    n = w.shape[1]
    return pl.pallas_call(
        _mm_kernel,
        grid=(m // tm, n // tn),
        in_specs=[pl.BlockSpec((tm, k), lambda i, j: (i, 0)),
                  pl.BlockSpec((k, tn), lambda i, j: (0, j))],
        out_specs=pl.BlockSpec((tm, tn), lambda i, j: (i, j)),
        out_shape=jax.ShapeDtypeStruct((m, n), F32),
        compiler_params=_cp(("parallel", "arbitrary")),
        name=name,
    )(x, w)


def _banded_kernel(*refs, nh, k_off, v_off, with_sink, with_lse):
    refs = list(refs)
    sink_ref = refs.pop(0) if with_sink else None
    q_ref, kc_ref, kp_ref, vc_ref, vp_ref = refs[:5]
    o_ref = refs[5]
    lse_ref = refs[6] if with_lse else None
    has_prev = pl.program_id(1) > 0
    row = lax.broadcasted_iota(I32, (BLK, 2 * BLK), 0)
    col = lax.broadcasted_iota(I32, (BLK, 2 * BLK), 1)
    dist = row + BLK - col
    valid = (dist >= 0) & (dist <= BLK) & (has_prev | (col >= BLK))
    q = q_ref[...]
    kc, kp, vc, vp = kc_ref[...], kp_ref[...], vc_ref[...], vp_ref[...]
    for h in range(nh):
        qh = q[:, h * HD:(h + 1) * HD]
        ko, vo = k_off[h], v_off[h]
        kk = jnp.concatenate([kp[:, ko:ko + HD], kc[:, ko:ko + HD]], axis=0)
        vv = jnp.concatenate([vp[:, vo:vo + HD], vc[:, vo:vo + HD]], axis=0)
        s = _dot_nt(qh, kk) * (1.0 / math.sqrt(HD))
        s = jnp.where(valid, s, -jnp.inf)
        m = jnp.max(s, axis=-1, keepdims=True)
        if with_sink:
            sk = sink_ref[h]
            m = jnp.maximum(m, sk)
        e = jnp.exp(s - m)
        den = jnp.sum(e, axis=-1, keepdims=True)
        if with_sink:
            den = den + jnp.exp(sk - m)
        o = _dot(e / den, vv)
        o_ref[:, h * HD:(h + 1) * HD] = o.astype(o_ref.dtype)
        if with_lse:
            lse_ref[:, h * HD:(h + 1) * HD] = jnp.broadcast_to(m + jnp.log(den), (BLK, HD))


def _banded_a(qkv, gi, dil):
    cols = jnp.concatenate([qkv[:NP, o + gi * A_W:o + (gi + 1) * A_W] for o in (0, 768, 1536)], axis=1)
    view = cols.astype(BF16).reshape(NP // dil, dil * 3 * A_W)
    nb = NP // dil // BLK
    cpr = 3

    def spec(off, prev):
        off = off // 3
        if prev:
            return pl.BlockSpec((BLK, A_W), lambda r, j: (jnp.maximum(j - 1, 0), r * cpr + off))
        return pl.BlockSpec((BLK, A_W), lambda r, j: (j, r * cpr + off))

    offs = tuple(h * HD for h in range(A_HEADS))
    out_spec = pl.BlockSpec((BLK, A_W), lambda r, j: (j, r))
    o, lse = pl.pallas_call(
        functools.partial(_banded_kernel, nh=A_HEADS, k_off=offs, v_off=offs,
                          with_sink=False, with_lse=True),
        grid=(dil, nb),
        in_specs=[spec(0, False), spec(3, False), spec(3, True), spec(6, False), spec(6, True)],
        out_specs=[out_spec, out_spec],
        out_shape=[jax.ShapeDtypeStruct((NP // dil, dil * A_W), F32)] * 2,
        compiler_params=_cp(("parallel", "arbitrary")),
        name=f"banded_a{gi}",
    )(view, view, view, view, view)
    return o.reshape(NP, A_W), lse.reshape(NP, A_W)


def _banded_c(qkv, sinks):
    nq = C_HEADS * HD
    kvw = 2 * C_KV * HD
    k_off = tuple((h // 4) * HD for h in range(C_HEADS))
    v_off = tuple(C_KV * HD + (h // 4) * HD for h in range(C_HEADS))
    cur = pl.BlockSpec((BLK, kvw), lambda r, j: (j, nq // kvw))
    prev = pl.BlockSpec((BLK, kvw), lambda r, j: (jnp.maximum(j - 1, 0), nq // kvw))
    return pl.pallas_call(
        functools.partial(_banded_kernel, nh=C_HEADS, k_off=k_off, v_off=v_off,
                          with_sink=True, with_lse=False),
        grid=(1, NP // BLK),
        in_specs=[pl.BlockSpec(memory_space=pltpu.SMEM),
                  pl.BlockSpec((BLK, nq), lambda r, j: (j, 0)), cur, prev, cur, prev],
        out_specs=pl.BlockSpec((BLK, nq), lambda r, j: (j, 0)),
        out_shape=jax.ShapeDtypeStruct((T, nq), BF16),
        compiler_params=_cp(("parallel", "arbitrary")),
        name="banded_c",
    )(sinks, qkv, qkv, qkv, qkv, qkv)


def _combine_kernel(o1, o2, o3, l1, l2, l3, out_ref):
    a, b, c = l1[...], l2[...], l3[...]
    m = jnp.maximum(jnp.maximum(a, b), c)
    ea, eb, ec = jnp.exp(a - m), jnp.exp(b - m), jnp.exp(c - m)
    tot = ea + eb + ec
    out_ref[...] = ((ea / tot) * o1[...] + (eb / tot) * o2[...] + (ec / tot) * o3[...]).astype(out_ref.dtype)


def _combine_groups(outs, lses):
    tm = 512
    spec = pl.BlockSpec((tm, A_W), lambda i: (i, 0))
    return pl.pallas_call(
        _combine_kernel,
        grid=(NP // tm,),
        in_specs=[spec] * 6,
        out_specs=spec,
        out_shape=jax.ShapeDtypeStruct((T, A_W), BF16),
        compiler_params=_cp(("parallel",)),
        name="combine_groups",
    )(*outs, *lses)


def _dec_scores(q, nq, nk, kt, knew, dil, sink_col):
    rep = nq // nk
    w = kt.shape[1]
    qrows = jnp.concatenate([q[:, h * HD:(h + 1) * HD] for h in range(nq)], axis=0)
    qt = jnp.concatenate([qrows] * nk, axis=1) if nk > 1 else qrows
    r_i = lax.broadcasted_iota(I32, qt.shape, 0)
    l_i = lax.broadcasted_iota(I32, qt.shape, 1)
    qbd = jnp.where((r_i // (S * rep)) == (l_i // HD), qt, 0.0)
    scale = 1.0 / math.sqrt(HD)
    sc = _dot(qbd, kt) * scale
    tok = lax.broadcasted_iota(I32, sc.shape, 0) % S
    back = lax.broadcasted_iota(I32, sc.shape, 1) - tok
    sc = jnp.where((back >= 0) & ((back & (dil - 1)) == 0), sc, -jnp.inf)
    sn = _dot_nt(qbd, knew) * scale
    back = lax.broadcasted_iota(I32, sn.shape, 0) % S - lax.broadcasted_iota(I32, sn.shape, 1)
    sn = jnp.where((back >= 0) & ((back & (dil - 1)) == 0), sn, -jnp.inf)
    m = jnp.maximum(jnp.max(sc, axis=1, keepdims=True), jnp.max(sn, axis=1, keepdims=True))
    if sink_col is not None:
        m = jnp.maximum(m, sink_col)
    ec, en = jnp.exp(sc - m), jnp.exp(sn - m)
    den = jnp.sum(ec, axis=1, keepdims=True) + jnp.sum(en, axis=1, keepdims=True)
    if sink_col is not None:
        den = den + jnp.exp(sink_col - m)
    return ec, en, m, den


def _pick_heads(o_all, nq, nk):
    rep = nq // nk
    cols = []
    for h in range(nq):
        kh = h // rep
        cols.append(o_all[h * S:(h + 1) * S, kh * HD:(kh + 1) * HD])
    return jnp.concatenate(cols, axis=1)


def _shift_cache(new_ref, old, kv_new):
    rows, w = old.shape
    nblk = w // BLK
    keep = BLK - S
    lane = lax.broadcasted_iota(I32, (rows, BLK), 1)
    left = [pltpu.roll(old[:, j * BLK:(j + 1) * BLK], keep, axis=1) for j in range(nblk)]
    tail = jnp.concatenate([jnp.zeros((keep, rows), F32), kv_new], axis=0).T
    left.append(tail)
    shape = new_ref.shape[:-1] + (BLK,)
    for j in range(nblk):
        new_ref[:, :, :, j * BLK:(j + 1) * BLK] = jnp.where(lane < keep, left[j], left[j + 1]).reshape(shape)


def _decode_attn_kernel(sink_ref, qa_ref, qc_ref, c1_ref, c2_ref, c3_ref, cc_ref, *rest):
    n_alias = len(rest) - 6
    n1_ref, n2_ref, n3_ref, nc_ref, oa_ref, oc_ref = rest[n_alias:]
    qa = qa_ref[...]
    qc = qc_ref[...]
    caches = (c1_ref, c2_ref, c3_ref)
    news = (n1_ref, n2_ref, n3_ref)
    grp = []
    for gi, (w, dil) in enumerate(A_PATTERN):
        q = qa[:, gi * A_W:(gi + 1) * A_W]
        knew = qa[:, 768 + gi * A_W:768 + (gi + 1) * A_W]
        vnew = qa[:, 1536 + gi * A_W:1536 + (gi + 1) * A_W]
        old = caches[gi][...].reshape(2 * A_W, w)
        ec, en, m, den = _dec_scores(q, A_HEADS, A_HEADS, old[0:A_W], knew, dil, None)
        grp.append((ec, en, m + jnp.log(den), den, old[A_W:2 * A_W], vnew))
        _shift_cache(news[gi], old, jnp.concatenate([knew, vnew], axis=1))
    lmax = functools.reduce(jnp.maximum, [g[2] for g in grp])
    gws = [jnp.exp(g[2] - lmax) for g in grp]
    gtot = functools.reduce(jnp.add, gws)
    o_all = None
    for gi, (ec, en, _, den, vt, vnew) in enumerate(grp):
        wcol = (gws[gi] / gtot) / den
        t = _dot_nt(ec * wcol, vt) + _dot(en * wcol, vnew)
        o_all = t if o_all is None else o_all + t
    oa_ref[...] = _pick_heads(o_all, A_HEADS, A_HEADS).astype(oa_ref.dtype)
    nqc = C_HEADS * HD
    kvw = C_KV * HD
    old = cc_ref[...].reshape(2 * kvw, BLK)
    knew = qc[:, nqc:nqc + kvw]
    vnew = qc[:, nqc + kvw:nqc + 2 * kvw]
    ec, en, m, den = _dec_scores(qc[:, 0:nqc], C_HEADS, C_KV, old[0:kvw], knew, 1, sink_ref[...])
    o_c = _dot_nt(ec / den, old[kvw:2 * kvw]) + _dot(en / den, vnew)
    oc_ref[...] = _pick_heads(o_c, C_HEADS, C_KV).astype(oc_ref.dtype)
    _shift_cache(nc_ref, old, qc[:, nqc:nqc + 2 * kvw])


def _decode_attn(layer, sink_col, qkv_a, qkv_c, caches, prev_out, oa_full, oc_full):
    row0 = NP // S

    def cache_spec(c):
        return pl.BlockSpec((None, None) + c.shape[2:], lambda b: (layer, b, 0, 0, 0, 0))

    in_specs = [pl.BlockSpec((C_HEADS * S, 1), lambda b: (0, 0)),
                pl.BlockSpec((S, QKV_A), lambda b: (row0 + b, 0)),
                pl.BlockSpec((S, QKV_C), lambda b: (row0 + b, 0))] + [cache_spec(c) for c in caches]
    args = [sink_col, qkv_a, qkv_c] + list(caches)
    aliases = {}
    if prev_out is not None:
        for k, p in enumerate(prev_out):
            in_specs.append(pl.BlockSpec(memory_space=pl.ANY))
            args.append(p)
            aliases[len(args) - 1] = k
    for k, full in enumerate((oa_full, oc_full)):
        in_specs.append(pl.BlockSpec(memory_space=pl.ANY))
        args.append(full)
        aliases[len(args) - 1] = len(caches) + k
    out_specs = [cache_spec(c) for c in caches] + [
        pl.BlockSpec((S, A_W), lambda b: (row0 + b, 0)),
        pl.BlockSpec((S, C_HEADS * HD), lambda b: (row0 + b, 0))]
    out_shape = [jax.ShapeDtypeStruct(c.shape, F32) for c in caches] + [
        jax.ShapeDtypeStruct((T, A_W), BF16), jax.ShapeDtypeStruct((T, C_HEADS * HD), BF16)]
    return pl.pallas_call(
        _decode_attn_kernel,
        grid=(NSEQ,),
        in_specs=in_specs,
        out_specs=out_specs,
        out_shape=out_shape,
        input_output_aliases=aliases,
        compiler_params=_cp(("arbitrary",), 56),
        name=f"decode_attn{layer}",
    )(*args)


def _ssd_kernel(xbcdt_ref, z_ref, conv0_ref, h0_ref, convw_ref, convb_ref, dtb_ref, arow_ref,
                dskip_ref, nrm_ref, *rest, q_len):
    y_ref, hout_ref, xp_ref, h_ref, yacc_ref = rest[-5:]
    c = pl.program_id(1)
    Q = q_len

    @pl.when(c == 0)
    def _():
        xp_ref[0:8, :] = conv0_ref[...]
        h_ref[...] = h0_ref[...]

    xbc = xbcdt_ref[:, 0:CONV_DIM]
    xp_ref[8:8 + Q, :] = xbc
    w = convw_ref[...]
    acc = convb_ref[...] + w[3:4, :] * xbc
    acc = acc + w[2:3, :] * xp_ref[7:7 + Q, :]
    acc = acc + w[1:2, :] * xp_ref[6:6 + Q, :]
    acc = acc + w[0:1, :] * xp_ref[5:5 + Q, :]
    u = _silu(acc)
    xp_ref[0:8, :] = xbc[Q - 8:Q, :]

    dtp = _softplus(xbcdt_ref[:, CONV_DIM:XBCDT_W] + dtb_ref[...])
    da = dtp * arow_ref[...]
    ri = lax.broadcasted_iota(I32, (Q, Q), 0)
    ci = lax.broadcasted_iota(I32, (Q, Q), 1)
    tri = ri >= ci
    acs = jnp.dot(tri.astype(F32), da, precision=lax.Precision.HIGHEST, preferred_element_type=F32)
    acs_t = acs.T
    dt_t = dtp.T
    hpg = SSM_HEADS // 2
    for g in range(2):
        bg = u[:, SSM_INNER + g * SSM_N:SSM_INNER + (g + 1) * SSM_N]
        cg = u[:, SSM_INNER + (2 + g) * SSM_N:SSM_INNER + (3 + g) * SSM_N]
        cb = _dot_nt(cg, bg)
        for hh in range(hpg):
            hd = g * hpg + hh
            a_col = acs[:, hd:hd + 1]
            seg = a_col - acs_t[hd:hd + 1, :]
            decay = jnp.exp(jnp.where(tri, seg, -jnp.inf))
            mh = cb * decay * dt_t[hd:hd + 1, :]
            xh = u[:, hd * HD:(hd + 1) * HD]
            hst = h_ref[hd]
            y = _dot(mh, xh) + _dot_nt(cg, hst) * jnp.exp(a_col)
            a_last = acs[Q - 1:Q, hd:hd + 1]
            tail = jnp.exp(a_last - a_col) * dtp[:, hd:hd + 1]
            if Q == BLK:
                upd = lax.dot_general(xh * tail, bg, (((0,), (0,)), ((), ())),
                                      precision=lax.Precision.HIGHEST, preferred_element_type=F32)
            else:
                upd = _dot_tn(xh * tail, bg)
            h_ref[hd] = hst * jnp.exp(a_last) + upd
            yacc_ref[:, hd * HD:(hd + 1) * HD] = y + dskip_ref[:, hd * HD:(hd + 1) * HD] * xh
    yz = yacc_ref[...] * _silu(z_ref[...])
    half = SSM_INNER // 2
    for g in range(2):
        yg = yz[:, g * half:(g + 1) * half]
        y_ref[:, g * half:(g + 1) * half] = (_rms(yg) * nrm_ref[:, g * half:(g + 1) * half]).astype(y_ref.dtype)

    @pl.when(c == pl.num_programs(1) - 1)
    def _():
        hout_ref[...] = h_ref[...]


def _ssd(layer, xbcdt, z, conv0, h0, convw, convb, dtb, arow, dskip, nrm, nseq, q_len, row0,
         y_prev=None, h_prev=None):
    nchunks = (NP if nseq == 1 else S) // q_len
    vec = lambda n: pl.BlockSpec((1, n), lambda b, c: (0, 0))
    state = pl.BlockSpec((None, None, SSM_HEADS, HD, SSM_N), lambda b, c: (layer, b, 0, 0, 0))
    in_specs = [pl.BlockSpec((q_len, XBCDT_W), lambda b, c: (row0 + b * nchunks + c, 0)),
                pl.BlockSpec((q_len, SSM_INNER), lambda b, c: (row0 + b * nchunks + c, 0)),
                pl.BlockSpec((None, 8, CONV_DIM), lambda b, c: (b, 0, 0)),
                state,
                pl.BlockSpec((4, CONV_DIM), lambda b, c: (0, 0)),
                vec(CONV_DIM), vec(128), vec(128), vec(SSM_INNER), vec(SSM_INNER)]
    args = [xbcdt, z, conv0, h0, convw, convb, dtb, arow, dskip, nrm]
    aliases = {}
    for out_idx, prev in ((0, y_prev), (1, h_prev)):
        if prev is not None:
            in_specs.append(pl.BlockSpec(memory_space=pl.ANY))
            args.append(prev)
            aliases[len(args) - 1] = out_idx
    return pl.pallas_call(
        functools.partial(_ssd_kernel, q_len=q_len),
        grid=(nseq, nchunks),
        in_specs=in_specs,
        out_specs=[pl.BlockSpec((q_len, SSM_INNER), lambda b, c: (row0 + b * nchunks + c, 0)), state],
        out_shape=[jax.ShapeDtypeStruct((T, SSM_INNER), BF16), jax.ShapeDtypeStruct(h0.shape, F32)],
        scratch_shapes=[pltpu.VMEM((q_len + 8, CONV_DIM), F32),
                        pltpu.VMEM((SSM_HEADS, HD, SSM_N), F32),
                        pltpu.VMEM((q_len, SSM_INNER), F32)],
        input_output_aliases=aliases,
        compiler_params=_cp(("arbitrary", "arbitrary")),
        name=f"ssd_q{q_len}",
    )(*args)


def _merge_kernel(h_ref, oa_ref, ob_ref, oc_ref, g0_ref, g1_ref, g2_ref, wa_ref, wb_ref, wc_ref, o_ref):
    h = h_ref[...]
    acc = _sigmoid(_dot(h, g0_ref[...])) * _dot(oa_ref[...], wa_ref[...])
    acc = acc + _sigmoid(_dot(h, g1_ref[...])) * _dot(ob_ref[...], wb_ref[...])
    acc = acc + _sigmoid(_dot(h, g2_ref[...])) * _dot(oc_ref[...], wc_ref[...])
    o_ref[...] = acc.astype(o_ref.dtype)


def _merge(layer, h1, oa, ob, oc, wgate, w_br_a, w_br_b, w_br_c):
    tm, tn = 1024, 512
    nj = D // tn
    row = lambda k: pl.BlockSpec((tm, k), lambda i, j: (i, 0))
    gate = lambda b: pl.BlockSpec((D, tn), lambda i, j: (0, b * nj + j))
    br = lambda k: pl.BlockSpec((None, k, tn), lambda i, j: (layer, 0, j))
    return pl.pallas_call(
        _merge_kernel,
        grid=(T // tm, nj),
        in_specs=[row(D), row(A_W), row(SSM_INNER), row(C_HEADS * HD), gate(0), gate(1), gate(2),
                  br(A_W), br(SSM_INNER), br(C_HEADS * HD)],
        out_specs=pl.BlockSpec((tm, tn), lambda i, j: (i, j)),
        out_shape=jax.ShapeDtypeStruct((T, D), BF16),
        compiler_params=_cp(("parallel", "arbitrary"), 56),
        name="merge",
    )(h1, oa, ob, oc, wgate, wgate, wgate, w_br_a, w_br_b, w_br_c)


def _outproj_kernel(m_ref, w_ref, x_ref, gpost_ref, gpre_ref, gap_ref, gas_ref, shp_ref, shs_ref,
                    scp_ref, scs_ref, x_out_ref, hf_ref, hp_ref):
    is_p = pl.program_id(0) < NPT_TOK
    ga = _sel_tile(is_p, gap_ref, gas_ref)
    sh = _sel_tile(is_p, shp_ref, shs_ref)
    sc = _sel_tile(is_p, scp_ref, scs_ref)
    mix = _dot(m_ref[...], w_ref[...])
    xn = x_ref[...] + ga * (_rms(mix) * gpost_ref[...])
    x_out_ref[...] = xn
    hf = (_rms(xn) * gpre_ref[...]) * (1.0 + sc) + sh
    hf_ref[...] = hf
    _store_token_tiles(hp_ref, _pack_bf16_pairs(hf))


def _outproj(merged, w_out_bf, x, g_post, g_pre, mod_p, mod_s):
    tok = pl.BlockSpec((TM_TOK, D), lambda i: (i, 0))
    vec = pl.BlockSpec((1, D), lambda i: (0, 0))
    return pl.pallas_call(
        _outproj_kernel,
        grid=(T // TM_TOK,),
        in_specs=[tok, pl.BlockSpec((D, D), lambda i: (0, 0)), tok, vec, vec]
        + _mod_specs(2) + _mod_specs(3) + _mod_specs(4),
        out_specs=[tok, tok, pl.BlockSpec((TM_TOK * XS_R, LANES), lambda i: (i, 0))],
        out_shape=[jax.ShapeDtypeStruct((T, D), F32)] * 2
        + [jax.ShapeDtypeStruct((T * XS_R, LANES), jnp.uint32)],
        compiler_params=_cp(("parallel",), 56),
        name="outproj",
    )(merged, w_out_bf, x, g_post.reshape(1, D), g_pre.reshape(1, D), mod_p, mod_s, mod_p, mod_s, mod_p, mod_s)


def _router_kernel(hf_ref, rwt_ref, bias_ref, topi_ref, topw_ref):
    tm = hf_ref.shape[0]
    logits = lax.dot_general(rwt_ref[...], hf_ref[...], (((1,), (1,)), ((), ())),
                             precision=lax.Precision.HIGHEST, preferred_element_type=F32)
    scores = _sigmoid(logits)
    sel = scores + bias_ref[...]
    ng = 8
    sel3 = sel.reshape(ng, N_EXP // ng, tm)
    idx3 = lax.broadcasted_iota(I32, sel3.shape, 1)
    m1 = jnp.max(sel3, axis=1, keepdims=True)
    first = jnp.min(jnp.where(sel3 == m1, idx3, N_EXP), axis=1, keepdims=True)
    m2 = jnp.max(jnp.where(idx3 == first, -jnp.inf, sel3), axis=1, keepdims=True)
    gs = m1 + m2
    gidx = lax.broadcasted_iota(I32, (ng, 1, tm), 0)
    gmask = jnp.zeros((ng, 1, tm), jnp.bool_)
    for _ in range(4):
        m = jnp.max(gs, axis=0, keepdims=True)
        f = jnp.min(jnp.where(gs == m, gidx, ng), axis=0, keepdims=True)
        hit = gidx == f
        gmask = jnp.logical_or(gmask, hit)
        gs = jnp.where(hit, -jnp.inf, gs)
    masked = jnp.where(gmask, sel3, -jnp.inf).reshape(N_EXP, tm)
    eidx = lax.broadcasted_iota(I32, (N_EXP, tm), 0)
    tis, tws = [], []
    for _ in range(TOP_K):
        m = jnp.max(masked, axis=0, keepdims=True)
        f = jnp.min(jnp.where(masked == m, eidx, N_EXP), axis=0, keepdims=True)
        hit = eidx == f
        tis.append(f)
        tws.append(jnp.sum(jnp.where(hit, scores, 0.0), axis=0, keepdims=True))
        masked = jnp.where(hit, -jnp.inf, masked)
    wsum = functools.reduce(jnp.add, tws)
    topi_ref[...] = jnp.concatenate(tis, axis=0)
    topw_ref[...] = jnp.concatenate([w / wsum * ROUTED_SCALE for w in tws], axis=0)


def _router(hf, rwt, bias_col):
    tm = 512
    return pl.pallas_call(
        _router_kernel,
        grid=(T // tm,),
        in_specs=[pl.BlockSpec((tm, D), lambda i: (i, 0)),
                  pl.BlockSpec((N_EXP, D), lambda i: (0, 0)),
                  pl.BlockSpec((N_EXP, 1), lambda i: (0, 0))],
        out_specs=[pl.BlockSpec((TOP_K, tm), lambda i: (0, i))] * 2,
        out_shape=[jax.ShapeDtypeStruct((TOP_K, T), I32), jax.ShapeDtypeStruct((TOP_K, T), F32)],
        compiler_params=_cp(("parallel",)),
        name="router",
    )(hf, rwt, bias_col)


def _route_tables(topi):
    onehot = topi[None] == jnp.arange(N_EXP, dtype=I32)[:, None, None]
    tok = jnp.any(onehot, axis=1).astype(I32)
    incl = jnp.cumsum(tok, axis=1)
    counts = incl[:, -1]
    padded = (counts + BM - 1) // BM * BM
    pad_end = jnp.cumsum(padded)
    base = (pad_end - padded)[:, None] + incl - tok
    dest = jnp.sum(jnp.where(onehot, base[:, None, :], 0), axis=0).astype(I32)
    starts = jnp.arange(NB, dtype=I32) * BM
    block_e = jnp.minimum(jnp.sum(pad_end[None, :] <= starts[:, None], axis=1), N_EXP - 1).astype(I32)
    n_used = (pad_end[-1] // BM).astype(I32).reshape(1)
    prev_e = jnp.concatenate([jnp.full((1,), -1, I32), block_e[:-1]])
    first = (block_e != prev_e).astype(I32)
    eid = jnp.arange(N_EXP, dtype=I32)
    later = jnp.where((counts[None, :] > 0) & (eid[None, :] > eid[:, None]), eid[None, :], N_EXP)
    nxt = jnp.min(later, axis=1)
    nxt = jnp.where(nxt == N_EXP, -1, nxt).astype(I32)
    next_e = jnp.sum(jnp.where(block_e[:, None] == eid[None, :], nxt[None, :], 0), axis=1).astype(I32)
    return dest, block_e, n_used, first, next_e


def _dispatch_kernel(dest_ref, hf_ref, xs_in_ref, xs_ref, sem):
    del xs_in_ref

    def body(t, carry):
        for k in range(TOP_K):
            src = hf_ref.at[pl.ds(pl.multiple_of(t * XS_R, XS_R), XS_R)]
            dst = xs_ref.at[pl.ds(pl.multiple_of(dest_ref[k, t] * XS_R, XS_R), XS_R)]
            pltpu.make_async_copy(src, dst, sem).start(priority=k % 2)
        return carry

    lax.fori_loop(0, TT_DISP, body, 0)
    for k in range(TOP_K):
        pltpu.make_async_copy(hf_ref, xs_ref.at[pl.ds(0, TT_DISP * XS_R)], sem).wait()


def _dispatch(dest, hf, xs_buf):
    return pl.pallas_call(
        _dispatch_kernel,
        grid=(T // TT_DISP,),
        in_specs=[pl.BlockSpec((TOP_K, TT_DISP), lambda i: (0, i), memory_space=pltpu.SMEM),
                  pl.BlockSpec((TT_DISP * XS_R, LANES), lambda i: (i, 0)),
                  pl.BlockSpec(memory_space=pl.ANY)],
        out_specs=pl.BlockSpec(memory_space=pl.ANY),
        out_shape=jax.ShapeDtypeStruct((P_ROWS * XS_R, LANES), jnp.uint32),
        scratch_shapes=[pltpu.SemaphoreType.DMA(())],
        input_output_aliases={2: 0},
        compiler_params=_cp(("arbitrary",)),
        name="moe_dispatch",
    )(dest, hf, xs_buf)


def _swiglu(x, wg, wu, wd):
    return _dot(_silu(_dot(x, wg)) * _dot(x, wu), wd)


def _expert_kernel(be_ref, nu_ref, first_ref, next_ref, x_ref, wg_hbm, wu_hbm, wd_hbm, o_ref,
                   stage_g, stage_u, stage_d, wg_bf, wu_bf, wd_bf, sems, *, layer):
    i = pl.program_id(0)
    used = i < nu_ref[0]

    def weight_copies(e):
        return [pltpu.make_async_copy(wg_hbm.at[layer, e], stage_g, sems.at[0]),
                pltpu.make_async_copy(wu_hbm.at[layer, e], stage_u, sems.at[1]),
                pltpu.make_async_copy(wd_hbm.at[layer, e], stage_d, sems.at[2])]

    @pl.when(i == 0)
    def _():
        for c in weight_copies(be_ref[0]):
            c.start()

    @pl.when(jnp.logical_and(used, first_ref[i] == 1))
    def _():
        for c in weight_copies(be_ref[i]):
            c.wait()
        wg_bf[...] = stage_g[...].astype(BF16)
        wu_bf[...] = stage_u[...].astype(BF16)
        wd_bf[...] = stage_d[...].astype(BF16)

        @pl.when(next_ref[i] >= 0)
        def _():
            for c in weight_copies(next_ref[i]):
                c.start()

    @pl.when(used)
    def _():
        x = _unpack_bf16_pairs(_load_token_tiles(x_ref, (), BM, XS_R))
        o_ref[...] = _swiglu(x, wg_bf[...], wu_bf[...], wd_bf[...])

    @pl.when(jnp.logical_not(used))
    def _():
        o_ref[...] = jnp.zeros_like(o_ref)


def _experts(layer, xs, block_e, n_used, first, next_e, exp_gate, exp_up, exp_down):
    hbm = pl.BlockSpec(memory_space=pl.ANY)
    return pl.pallas_call(
        functools.partial(_expert_kernel, layer=layer),
        grid_spec=pltpu.PrefetchScalarGridSpec(
            num_scalar_prefetch=4,
            grid=(NB,),
            in_specs=[pl.BlockSpec((BM * XS_R, LANES), lambda i, *_: (i, 0)), hbm, hbm, hbm],
            out_specs=pl.BlockSpec((BM, D), lambda i, *_: (i, 0)),
            scratch_shapes=[pltpu.VMEM((D, MOE_H), F32), pltpu.VMEM((D, MOE_H), F32),
                            pltpu.VMEM((MOE_H, D), F32),
                            pltpu.VMEM((D, MOE_H), BF16), pltpu.VMEM((D, MOE_H), BF16),
                            pltpu.VMEM((MOE_H, D), BF16),
                            pltpu.SemaphoreType.DMA((3,))]),
        out_shape=jax.ShapeDtypeStruct((P_ROWS, D), F32),
        compiler_params=_cp(("arbitrary",), 48),
        name="moe_experts",
    )(block_e, n_used, first, next_e, xs, exp_gate, exp_up, exp_down)


def _shared_kernel(x_ref, wg_ref, wu_ref, wd_ref, o_ref):
    o_ref[...] = _swiglu(x_ref[...], wg_ref[...], wu_ref[...], wd_ref[...])


def _shared_expert(layer, hf, wg, wu, wd):
    tm = 512
    wspec = lambda a, b: pl.BlockSpec((None, a, b), lambda i: (layer, 0, 0))
    return pl.pallas_call(
        _shared_kernel,
        grid=(T // tm,),
        in_specs=[pl.BlockSpec((tm, D), lambda i: (i, 0)), wspec(D, MOE_H), wspec(D, MOE_H), wspec(MOE_H, D)],
        out_specs=pl.BlockSpec((tm, D), lambda i: (i, 0)),
        out_shape=jax.ShapeDtypeStruct((T, D), F32),
        compiler_params=_cp(("parallel",), 56),
        name="moe_shared",
    )(hf, wg, wu, wd)


def _moe_combine_kernel(dest_ref, dnext_ref, ys_ref, w_ref, fsh_ref, x_ref, gpost_ref, gap_ref, gas_ref,
                        o_ref, gbuf, sems):
    tt = TT_DISP
    i = pl.program_id(0)
    is_p = i < NP // tt
    slot = i % 2

    def gather(d_ref, s):
        def body(t, carry):
            for k in range(TOP_K):
                pltpu.make_async_copy(ys_ref.at[pl.ds(d_ref[k, t], 1)], gbuf.at[s, k, pl.ds(t, 1)],
                                      sems.at[s]).start(priority=k % 2)
            return carry
        lax.fori_loop(0, tt, body, 0)

    @pl.when(i == 0)
    def _():
        gather(dest_ref, 0)

    @pl.when(i + 1 < pl.num_programs(0))
    def _():
        gather(dnext_ref, 1 - slot)

    for k in range(TOP_K):
        pltpu.make_async_copy(ys_ref.at[pl.ds(0, tt)], gbuf.at[slot, k], sems.at[slot]).wait()
    w = w_ref[...]
    f = fsh_ref[...]
    for k in range(TOP_K):
        f = f + gbuf[slot, k] * w[:, k:k + 1]
    ga = jnp.where(is_p, gap_ref[0:1, :], gas_ref[...])
    o_ref[...] = x_ref[...] + ga * (_rms(f) * gpost_ref[...])


def _moe_combine(dest, ys, topw_t, f_shared, x, g_post, mod_p, mod_s):
    tt = TT_DISP
    tok = pl.BlockSpec((tt, D), lambda i: (i, 0))
    return pl.pallas_call(
        _moe_combine_kernel,
        grid=(T // tt,),
        in_specs=[pl.BlockSpec((TOP_K, tt), lambda i: (0, i), memory_space=pltpu.SMEM),
                  pl.BlockSpec((TOP_K, tt), lambda i: (0, jnp.minimum(i + 1, T // tt - 1)),
                               memory_space=pltpu.SMEM),
                  pl.BlockSpec(memory_space=pl.ANY),
                  pl.BlockSpec((tt, TOP_K), lambda i: (i, 0)),
                  tok, tok, pl.BlockSpec((1, D), lambda i: (0, 0)),
                  pl.BlockSpec((8, D), lambda i: (0, 5)),
                  pl.BlockSpec((tt, D), lambda i: (jnp.maximum(i - NP // tt, 0), 5))],
        out_specs=tok,
        out_shape=jax.ShapeDtypeStruct((T, D), F32),
        scratch_shapes=[pltpu.VMEM((2, TOP_K, tt, D), F32), pltpu.SemaphoreType.DMA((2,))],
        compiler_params=_cp(("arbitrary",)),
        name="moe_combine",
    )(dest, dest, ys, topw_t, f_shared, x, g_post.reshape(1, D), mod_p, mod_s)


def _pad_cols(w, n):
    return jnp.pad(w, ((0, 0), (0, n - w.shape[1])))


def _layer(l, x, mod_p, mod_s, p, dec_caches, dec_prev, ssm_prev, xs_buf):
    w_in = p['w_in'][l]
    h1 = _prenorm(x, p['norm_pre_mix'][l], mod_p, mod_s)
    qkv_a = _matmul(h1, w_in[:, OFF_QKV_A:OFF_QKV_C].astype(BF16), 1024, 768, "proj_qkv_a")
    qkv_c = _matmul(h1, w_in[:, OFF_QKV_C:OFF_Z].astype(BF16), 1024, QKV_C, "proj_qkv_c")
    z = _matmul(h1, w_in[:, OFF_Z:OFF_XBC].astype(BF16), 1024, SSM_INNER, "proj_z")
    xbcdt = _matmul(h1, _pad_cols(w_in[:, OFF_XBC:OFF_GATES], XBCDT_W).astype(BF16), 1024, XBCDT_W, "proj_xbcdt")

    outs, lses = zip(*[_banded_a(qkv_a, gi, dil) for gi, (_, dil) in enumerate(A_PATTERN)])
    oa_p = _combine_groups(outs, lses)
    oc_p = _banded_c(qkv_c, p['sinks'][l])
    sink_col = jnp.repeat(p['sinks'][l], S).reshape(C_HEADS * S, 1)
    n1, n2, n3, ncc, oa, oc = _decode_attn(l, sink_col, qkv_a, qkv_c, dec_caches, dec_prev, oa_p, oc_p)

    dtb = jnp.pad(p['dt_bias'][l], (0, 128 - SSM_HEADS)).reshape(1, 128)
    arow = jnp.pad(-jnp.exp(p['a_log'][l]), (0, 128 - SSM_HEADS)).reshape(1, 128)
    dskip = jnp.repeat(p['d_skip'][l], HD).reshape(1, SSM_INNER)
    nrm = p['ssm_norm'][l].reshape(1, SSM_INNER)
    convw = p['conv_w'][l]
    convb = p['conv_b'][l].reshape(1, CONV_DIM)
    ob_p, ssm_p = _ssd(0, xbcdt, z, jnp.zeros((1, 8, CONV_DIM), F32),
                       jnp.zeros((1, 1, SSM_HEADS, HD, SSM_N), F32),
                       convw, convb, dtb, arow, dskip, nrm, 1, BLK, 0)
    conv0_s = jnp.pad(p['state_conv'][l], ((0, 0), (5, 0), (0, 0)))
    ob, ssm_s = _ssd(l, xbcdt, z, conv0_s, p['state_ssm'],
                     convw, convb, dtb, arow, dskip, nrm, NSEQ, S, NP // S, y_prev=ob_p, h_prev=ssm_prev)
    merged = _merge(l, h1, oa, ob, oc, w_in[:, OFF_GATES:].astype(BF16), p['w_br_a'], p['w_br_b'], p['w_br_c'])
    x, hf, hf_packed = _outproj(merged, p['w_out'][l].astype(BF16), x, p['norm_post_mix'][l],
                                p['norm_pre_ffn'][l], mod_p, mod_s)

    topi, topw = _router(hf, p['router_w'][l].T, p['router_bias'][l].reshape(N_EXP, 1))
    dest, block_e, n_used, first, next_e = _route_tables(topi)
    xs = _dispatch(dest, hf_packed, xs_buf)
    ys = _experts(l, xs, block_e, n_used, first, next_e, p['exp_gate'], p['exp_up'], p['exp_down'])
    f_shared = _shared_expert(l, hf, p['shared_gate'], p['shared_up'], p['shared_down'])
    x = _moe_combine(dest, ys, topw.T, f_shared, x, p['norm_post_ffn'][l], mod_p, mod_s)

    def last_kv(src, k_off, v_off, rows, heads):
        k = src[NP - rows:NP, k_off:k_off + heads * HD].reshape(rows, heads, HD)
        v = src[NP - rows:NP, v_off:v_off + heads * HD].reshape(rows, heads, HD)
        return jnp.stack([k, v], axis=1)[None]

    a_states = [last_kv(qkv_a, 768 + gi * A_W, 1536 + gi * A_W, w, A_HEADS)
                for gi, (w, _) in enumerate(A_PATTERN)]
    c_state = last_kv(qkv_c, C_HEADS * HD, C_HEADS * HD + C_KV * HD, BLK, C_KV)
    conv_p = xbcdt[NP - 3:NP, :CONV_DIM][None]
    conv_s = xbcdt[NP:, :CONV_DIM].reshape(NSEQ, S, CONV_DIM)[:, S - 3:]
    prompt_state = (*a_states, c_state, ssm_p.reshape(1, 2, SSM_HEADS // 2, HD, SSM_N), conv_p)
    return x, prompt_state, (n1, n2, n3, ncc), ssm_s, conv_s, xs


def kernel(x_prompt, x_sample, cache_a1_kv, cache_a2_kv, cache_a3_kv, cache_c_kv, state_ssm, state_conv,
           c_prompt, c_sample, ada_w, ada_b, norm_pre_mix, norm_post_mix, norm_pre_ffn, norm_post_ffn,
           w_in, conv_w, conv_b, dt_bias, a_log, d_skip, ssm_norm, sinks, w_br_a, w_br_b, w_br_c, w_out,
           router_w, router_bias, exp_gate, exp_up, exp_down, shared_gate, shared_up, shared_down):
    p = dict(norm_pre_mix=norm_pre_mix, norm_post_mix=norm_post_mix, norm_pre_ffn=norm_pre_ffn,
             norm_post_ffn=norm_post_ffn, w_in=w_in, conv_w=conv_w, conv_b=conv_b, dt_bias=dt_bias,
             a_log=a_log, d_skip=d_skip, ssm_norm=ssm_norm, sinks=sinks, w_br_a=w_br_a, w_br_b=w_br_b,
             w_br_c=w_br_c, w_out=w_out, router_w=router_w, router_bias=router_bias, exp_gate=exp_gate,
             exp_up=exp_up, exp_down=exp_down, shared_gate=shared_gate, shared_up=shared_up,
             shared_down=shared_down, state_conv=state_conv,
             state_ssm=state_ssm.reshape(DEPTH, NSEQ, SSM_HEADS, HD, SSM_N))
    x = jnp.concatenate([x_prompt.reshape(NP, D), x_sample.reshape(NS, D)], axis=0)
    c_all = jnp.concatenate([c_prompt, c_sample, jnp.zeros((7, D), F32)], axis=0)
    mod = _ada(c_all, ada_w, ada_b)
    dec_caches = tuple(jnp.transpose(c, (0, 1, 3, 4, 5, 2))
                       for c in (cache_a1_kv, cache_a2_kv, cache_a3_kv, cache_c_kv))
    xs_buf = jnp.zeros((P_ROWS * XS_R, LANES), jnp.uint32)
    dec_prev = ssm_s = None
    prompt_states, conv_s = [], []
    for l in range(DEPTH):
        mod_s = jnp.repeat(mod[l, 1:1 + NSEQ], S, axis=0)
        x, pst, dec_prev, ssm_s, conv_l, xs_buf = _layer(l, x, mod[l], mod_s, p, dec_caches, dec_prev,
                                                          ssm_s, xs_buf)
        prompt_states.append(pst)
        conv_s.append(conv_l)
    a1_p, a2_p, a3_p, c_p, ssm_p, conv_p = (jnp.stack(t) for t in zip(*prompt_states))
    n1, n2, n3, ncc = (jnp.transpose(c, (0, 1, 5, 2, 3, 4)) for c in dec_prev)
    y_p = x[:NP].reshape(1, NP, D)
    y_s = x[NP:].reshape(NSEQ, S, D)
    return (y_p, y_s, a1_p, n1, a2_p, n2, a3_p, n3, c_p, ncc,
            ssm_p, ssm_s.reshape(state_ssm.shape), conv_p, jnp.stack(conv_s))
```
